```python
import math
import jax, jax.numpy as jnp
from jax import lax
import numpy as np

D_MODEL = 2048
BATCH = 16
SEQ = 256
DEPTH = 1
DEC_BATCH = 8
DEC_SEQ = 2048
PAST_LEN = 256

GRID_W = 64
MLA_HEADS = 8
QK_NOPE = 128
QK_ROPE = 64
V_HEAD = 128
Q_LORA = 512
KV_LORA = 256
MLA_WIDTH = MLA_HEADS * V_HEAD
ROPE_AXIS_FREQS = QK_ROPE // 4
ROPE_THETA = 10000.0
Q_BLOCK = 128
RWKV_HEADS = 16
RWKV_HEAD = 64
RWKV_WIDTH = RWKV_HEADS * RWKV_HEAD
DECAY_LORA = 64
ICLR_LORA = 64
GATE_LORA = 128
MIX_WIDTH = MLA_WIDTH + RWKV_WIDTH
OFF_KV = Q_LORA
OFF_KR = OFF_KV + KV_LORA
OFF_RW = OFF_KR + QK_ROPE
RW_COLS = 3 * RWKV_WIDTH + DECAY_LORA + ICLR_LORA + GATE_LORA
IN_COLS = OFF_RW + RW_COLS
D_FF = ((8 * D_MODEL + 3 * 256 - 1) // (3 * 256)) * 256
LN_EPS = 1e-5
RMS_EPS = 1e-6
GN_EPS = 64e-5
ALPHA = (2.0 * DEPTH) ** 0.25
BETA = (8.0 * DEPTH) ** -0.25

kernel_name = "hymba_mla_rwkv7_flow_step"

F32 = jnp.float32


def _layer_norm(x, g, b):
    xf = x.astype(F32)
    mu = jnp.mean(xf, -1, keepdims=True)
    var = jnp.mean(jnp.square(xf - mu), -1, keepdims=True)
    return ((xf - mu) * lax.rsqrt(var + LN_EPS) * g + b).astype(x.dtype)


def _rms_norm(x, g):
    xf = x.astype(F32)
    return (xf * lax.rsqrt(jnp.mean(xf * xf, -1, keepdims=True) + RMS_EPS) * g).astype(x.dtype)


def _adaln(cond, w_mod, b_mod):
    m = jax.nn.silu(cond) @ w_mod + b_mod
    return jnp.split(m[:, None, :], 6, axis=-1)


def _post_norm(x, gate, f, g, b):
    return _layer_norm(ALPHA * x + gate * f, g, b)


def _axial_rope_angles(n_tokens):
    rows = n_tokens // GRID_W
    row = jnp.repeat(jnp.arange(rows), GRID_W).astype(F32)
    col = jnp.tile(jnp.arange(GRID_W), rows).astype(F32)
    freqs = ROPE_THETA ** (-jnp.arange(ROPE_AXIS_FREQS, dtype=F32) / ROPE_AXIS_FREQS)
    ang = jnp.concatenate([row[:, None] * freqs, col[:, None] * freqs], -1)
    return jnp.cos(ang), jnp.sin(ang)


def _apply_rope(x, cos, sin):
    half = QK_ROPE // 2
    x1, x2 = x[..., :half].astype(F32), x[..., half:].astype(F32)
    return jnp.concatenate([x1 * cos - x2 * sin, x2 * cos + x1 * sin], -1).astype(x.dtype)


def _mla_queries(q_down, p):
    b, n, _ = q_down.shape
    q = (_rms_norm(q_down, p["q_norm_g"]) @ p["w_uq"]).reshape(b, n, MLA_HEADS, QK_NOPE + QK_ROPE)
    return q[..., :QK_NOPE], q[..., QK_NOPE:]


def _mla_kv(ckv, p):
    b, n, _ = ckv.shape
    k_nope = (ckv @ p["w_uk"]).reshape(b, n, MLA_HEADS, QK_NOPE)
    v = (ckv @ p["w_uv"]).reshape(b, n, MLA_HEADS, V_HEAD)
    return k_nope, v


def _attend(q_nope, q_rope, k_nope, k_rope, v):
    scale = 1.0 / math.sqrt(QK_NOPE + QK_ROPE)
    s = jnp.einsum("bqhd,bkhd->bhqk", q_nope, k_nope) + jnp.einsum("bqhd,bkd->bhqk", q_rope, k_rope)
    pr = jax.nn.softmax(s.astype(F32) * scale, axis=-1).astype(v.dtype)
    return jnp.einsum("bhqk,bkhd->bqhd", pr, v)


def _attend_blocked(q_nope, q_rope, k_nope, k_rope, v):
    b, n, h, _ = q_nope.shape
    nb = n // Q_BLOCK
    to_blocks = lambda t: jnp.moveaxis(t.reshape(b, nb, Q_BLOCK, *t.shape[2:]), 1, 0)
    out = lax.map(lambda qs: _attend(qs[0], qs[1], k_nope, k_rope, v), (to_blocks(q_nope), to_blocks(q_rope)))
    return jnp.moveaxis(out, 0, 1).reshape(b, n, h, V_HEAD)


def _centred_shift(u, mu):
    prev = jnp.pad(u[:, :-1], ((0, 0), (1, 0), (0, 0)))
    nxt = jnp.pad(u[:, 1:], ((0, 0), (0, 1), (0, 0)))
    return u + mu * (0.5 * (prev + nxt) - u)


def _wkv_scan(s0, r, decay, k, v, kk, a, reverse):
    to_t = lambda t: jnp.moveaxis(t.astype(F32), 1, 0)

    def step(S, inp):
        r_t, w_t, k_t, v_t, kk_t, a_t = inp
        sa = jnp.einsum("bhvk,bhk->bhv", S, -kk_t)
        S = S * w_t[:, :, None, :] + sa[..., None] * (kk_t * a_t)[:, :, None, :] + v_t[..., None] * k_t[:, :, None, :]
        return S, jnp.einsum("bhvk,bhk->bhv", S, r_t)

    s_fin, ys = lax.scan(step, s0.astype(F32), (to_t(r), to_t(decay), to_t(k), to_t(v), to_t(kk), to_t(a)), reverse=reverse)
    return s_fin, jnp.moveaxis(ys, 0, 1)


def _rwkv_mixer(u, s0_fwd, s0_bwd, p):
    b, n, _ = u.shape
    u = _centred_shift(u, p["tok_shift_mu"])
    W = RWKV_WIDTH
    r, k, v = u[..., :W], u[..., W:2 * W], u[..., 2 * W:3 * W]
    wd = u[..., 3 * W:3 * W + DECAY_LORA]
    ad = u[..., 3 * W + DECAY_LORA:3 * W + DECAY_LORA + ICLR_LORA]
    gd = u[..., 3 * W + DECAY_LORA + ICLR_LORA:]
    heads = lambda t: t.reshape(b, n, RWKV_HEADS, RWKV_HEAD)
    kk = heads(k * p["k_k"]).astype(F32)
    kk = kk * lax.rsqrt(jnp.maximum(jnp.sum(kk * kk, -1, keepdims=True), 1e-24))
    g = jax.nn.sigmoid(gd) @ p["g_up"]
    r_h, v_h = heads(r), heads(v)
    r_k = p["r_k"].reshape(RWKV_HEADS, RWKV_HEAD)
    ys, bonuses, states = [], [], []
    for w0, w_up, a0, a_up, s0, rev in ((p["w0_fwd"], p["w_up_fwd"], p["a0_fwd"], p["a_up_fwd"], s0_fwd, False),
                                       (p["w0_bwd"], p["w_up_bwd"], p["a0_bwd"], p["a_up_bwd"], s0_bwd, True)):
        logw = -jax.nn.softplus(-(w0 + jnp.tanh(wd) @ w_up).astype(F32)) - 0.5
        decay = jnp.exp(-jnp.exp(logw))
        a = jax.nn.sigmoid(a0 + ad @ a_up)
        k_dir = heads(k * (1 + (a - 1) * p["k_a"]))
        s_fin, y_dir = _wkv_scan(s0, r_h, heads(decay), k_dir, v_h, kk, heads(a), rev)
        ys.append(y_dir)
        states.append(s_fin)
        bonuses.append(jnp.sum((r_h * k_dir * r_k).astype(F32), -1, keepdims=True) * v_h.astype(F32))
    y = ys[0] + ys[1]
    mu = jnp.mean(y, -1, keepdims=True)
    var = jnp.mean(jnp.square(y - mu), -1, keepdims=True)
    y = ((y - mu) * lax.rsqrt(var + GN_EPS)).reshape(b, n, W) * p["gn_g"] + p["gn_b"]
    y = (y + (bonuses[0] + bonuses[1]).reshape(b, n, W)) * g
    return y.astype(u.dtype), states[0], states[1]


def _merge(att, rw, p):
    b, n = rw.shape[:2]
    return jnp.concatenate([att.reshape(b, n, MLA_WIDTH), rw], -1) @ p["w_out"]


def _swiglu(h, p):
    return (jax.nn.silu(h @ p["w_ffn_gate"]) * (h @ p["w_ffn_up"])) @ p["w_ffn_down"]


def _context_layer(x, c_ctx, p):
    b, n, _ = x.shape
    sh1, sc1, g1, sh2, sc2, g2 = _adaln(c_ctx[None, :], p["w_mod"], p["b_mod"])
    proj = (x * (1 + sc1) + sh1) @ p["w_in"]
    q_nope, q_rope = _mla_queries(proj[..., :OFF_KV], p)
    ckv = _rms_norm(proj[..., OFF_KV:OFF_KR], p["kv_norm_g"])
    k_rope = proj[..., OFF_KR:OFF_RW]
    k_nope, v = _mla_kv(ckv, p)
    att = _attend(q_nope, q_rope, k_nope, k_rope, v)
    zeros = jnp.zeros((b, RWKV_HEADS, RWKV_HEAD, RWKV_HEAD), F32)
    rw, s_fwd, s_bwd = _rwkv_mixer(proj[..., OFF_RW:], zeros, zeros, p)
    x = _post_norm(x, g1, _merge(att, rw, p), p["ln1_g"], p["ln1_b"])
    x = _post_norm(x, g2, _swiglu(x * (1 + sc2) + sh2, p), p["ln2_g"], p["ln2_b"])
    return x, ckv, k_rope, s_fwd, s_bwd


def _latent_layer(x, c, ckv_ctx, krope_ctx, s_fwd, s_bwd, p):
    b, n, _ = x.shape
    sh1, sc1, g1, sh2, sc2, g2 = _adaln(c, p["w_mod"], p["b_mod"])
    proj = (x * (1 + sc1) + sh1) @ p["w_in"]
    cos, sin = _axial_rope_angles(n)
    q_nope, q_rope = _mla_queries(proj[..., :OFF_KV], p)
    q_rope = _apply_rope(q_rope, cos[:, None, :], sin[:, None, :])
    ckv = _rms_norm(proj[..., OFF_KV:OFF_KR], p["kv_norm_g"])
    k_rope = _apply_rope(proj[..., OFF_KR:OFF_RW], cos, sin)
    k_nope, v = _mla_kv(ckv, p)
    k_nope_c, v_c = _mla_kv(ckv_ctx, p)
    att = _attend_blocked(q_nope, q_rope,
                          jnp.concatenate([k_nope, k_nope_c], 1),
                          jnp.concatenate([k_rope, krope_ctx.astype(k_rope.dtype)], 1),
                          jnp.concatenate([v, v_c], 1))
    rw, _, _ = _rwkv_mixer(proj[..., OFF_RW:], s_fwd, s_bwd, p)
    x = _post_norm(x, g1, _merge(att, rw, p), p["ln1_g"], p["ln1_b"])
    x = _post_norm(x, g2, _swiglu(x * (1 + sc2) + sh2, p), p["ln2_g"], p["ln2_b"])
    return x


def setup_inputs(seed: int = 0) -> dict:
    key = jax.random.key(seed)
    ks = iter(jax.random.split(key, 48))
    nrm = lambda shape, scale: jax.random.normal(next(ks), shape, F32) * scale
    uni = lambda shape, lo, hi: jax.random.uniform(next(ks), shape, F32, lo, hi)
    L = DEPTH
    return {
        "x_prompt": nrm((BATCH, SEQ, D_MODEL), 1.0),
        "x_sample": nrm((DEC_BATCH, DEC_SEQ, D_MODEL), 1.0),
        "cache_ckv": nrm((DEC_BATCH, L, PAST_LEN, KV_LORA), 1.0),
        "cache_krope": nrm((DEC_BATCH, L, PAST_LEN, QK_ROPE), 1.0),
        "state_wkv_fwd": nrm((DEC_BATCH, L, RWKV_HEADS, RWKV_HEAD, RWKV_HEAD), 0.5),
        "state_wkv_bwd": nrm((DEC_BATCH, L, RWKV_HEADS, RWKV_HEAD, RWKV_HEAD), 0.5),
        "c": nrm((DEC_BATCH, D_MODEL), 1.0),
        "c_ctx": nrm((D_MODEL,), 1.0),
        "w_mod": nrm((L, D_MODEL, 6 * D_MODEL), 0.5 * D_MODEL ** -0.5),
        "b_mod": nrm((L, 6 * D_MODEL), 0.02),
        "w_in": nrm((L, D_MODEL, IN_COLS), D_MODEL ** -0.5),
        "q_norm_g": 1.0 + nrm((L, Q_LORA), 0.1),
        "kv_norm_g": 1.0 + nrm((L, KV_LORA), 0.1),
        "w_uq": nrm((L, Q_LORA, MLA_HEADS * (QK_NOPE + QK_ROPE)), Q_LORA ** -0.5),
        "w_uk": nrm((L, KV_LORA, MLA_HEADS * QK_NOPE), KV_LORA ** -0.5),
        "w_uv": nrm((L, KV_LORA, MLA_HEADS * V_HEAD), KV_LORA ** -0.5),
        "tok_shift_mu": uni((L, RW_COLS), 0.0, 1.0),
        "w0_fwd": uni((L, RWKV_WIDTH), -4.0, 1.0),
        "w_up_fwd": nrm((L, DECAY_LORA, RWKV_WIDTH), 0.5 * DECAY_LORA ** -0.5),
        "a0_fwd": nrm((L, RWKV_WIDTH), 0.1),
        "a_up_fwd": nrm((L, ICLR_LORA, RWKV_WIDTH), 0.5 * ICLR_LORA ** -0.5),
        "w0_bwd": uni((L, RWKV_WIDTH), -4.0, 1.0),
        "w_up_bwd": nrm((L, DECAY_LORA, RWKV_WIDTH), 0.5 * DECAY_LORA ** -0.5),
        "a0_bwd": nrm((L, RWKV_WIDTH), 0.1),
        "a_up_bwd": nrm((L, ICLR_LORA, RWKV_WIDTH), 0.5 * ICLR_LORA ** -0.5),
        "g_up": nrm((L, GATE_LORA, RWKV_WIDTH), GATE_LORA ** -0.5),
        "k_k": 0.85 + nrm((L, RWKV_WIDTH), 0.05),
        "k_a": 1.0 + nrm((L, RWKV_WIDTH), 0.05),
        "r_k": nrm((L, RWKV_WIDTH), 0.1),
        "gn_g": 1.0 + nrm((L, RWKV_WIDTH), 0.1),
        "gn_b": nrm((L, RWKV_WIDTH), 0.02),
        "w_out": nrm((L, MIX_WIDTH, D_MODEL), BETA * MIX_WIDTH ** -0.5),
        "ln1_g": 1.0 + nrm((L, D_MODEL), 0.1),
        "ln1_b": nrm((L, D_MODEL), 0.02),
        "w_ffn_gate": nrm((L, D_MODEL, D_FF), D_MODEL ** -0.5),
        "w_ffn_up": nrm((L, D_MODEL, D_FF), D_MODEL ** -0.5),
        "w_ffn_down": nrm((L, D_FF, D_MODEL), BETA * D_FF ** -0.5),
        "ln2_g": 1.0 + nrm((L, D_MODEL), 0.1),
        "ln2_b": nrm((L, D_MODEL), 0.02),
    }


def reference(x_prompt, x_sample, cache_ckv, cache_krope, state_wkv_fwd, state_wkv_bwd, c, c_ctx,
              w_mod, b_mod, w_in, q_norm_g, kv_norm_g, w_uq, w_uk, w_uv, tok_shift_mu,
              w0_fwd, w_up_fwd, a0_fwd, a_up_fwd, w0_bwd, w_up_bwd, a0_bwd, a_up_bwd,
              g_up, k_k, k_a, r_k, gn_g, gn_b, w_out, ln1_g, ln1_b,
              w_ffn_gate, w_ffn_up, w_ffn_down, ln2_g, ln2_b):
    weights = dict(w_mod=w_mod, b_mod=b_mod, w_in=w_in, q_norm_g=q_norm_g, kv_norm_g=kv_norm_g,
                   w_uq=w_uq, w_uk=w_uk, w_uv=w_uv, tok_shift_mu=tok_shift_mu,
                   w0_fwd=w0_fwd, w_up_fwd=w_up_fwd, a0_fwd=a0_fwd, a_up_fwd=a_up_fwd,
                   w0_bwd=w0_bwd, w_up_bwd=w_up_bwd, a0_bwd=a0_bwd, a_up_bwd=a_up_bwd,
                   g_up=g_up, k_k=k_k, k_a=k_a, r_k=r_k, gn_g=gn_g, gn_b=gn_b, w_out=w_out,
                   ln1_g=ln1_g, ln1_b=ln1_b, w_ffn_gate=w_ffn_gate, w_ffn_up=w_ffn_up,
                   w_ffn_down=w_ffn_down, ln2_g=ln2_g, ln2_b=ln2_b)
    y_prompt, y_sample = x_prompt, x_sample
    ckv_list, krope_list, sf_list, sb_list = [], [], [], []
    for l in range(DEPTH):
        p = {name: arr[l] for name, arr in weights.items()}
        y_prompt, ckv, krope, s_f, s_b = _context_layer(y_prompt, c_ctx, p)
        ckv_list.append(ckv)
        krope_list.append(krope)
        sf_list.append(s_f)
        sb_list.append(s_b)
        y_sample = _latent_layer(y_sample, c, cache_ckv[:, l], cache_krope[:, l],
                                 state_wkv_fwd[:, l], state_wkv_bwd[:, l], p)
    new_ckv = jnp.stack(ckv_list, 1)
    new_krope = jnp.stack(krope_list, 1)
    new_state_fwd = jnp.stack(sf_list, 1)
    new_state_bwd = jnp.stack(sb_list, 1)
    return (y_prompt, y_sample, new_ckv, new_krope, new_state_fwd, new_state_bwd)
```

```python
import functools
import math

import jax
import jax.numpy as jnp
from jax import lax
from jax.experimental import pallas as pl
from jax.experimental.pallas import tpu as pltpu

F32 = jnp.float32
BF16 = jnp.bfloat16

D_MODEL = 2048
GRID_W = 64
MLA_HEADS = 8
QK_NOPE = 128
QK_ROPE = 64
V_HEAD = 128
Q_LORA = 512
KV_LORA = 256
ROPE_AXIS_FREQS = QK_ROPE // 4
ROPE_THETA = 10000.0
RWKV_HEADS = 16
RWKV_HEAD = 64
RWKV_WIDTH = RWKV_HEADS * RWKV_HEAD
DECAY_LORA = 64
ICLR_LORA = 64
GATE_LORA = 128
OFF_KV = Q_LORA
OFF_KR = OFF_KV + KV_LORA
OFF_RW = OFF_KR + QK_ROPE
RW_COLS = 3 * RWKV_WIDTH + DECAY_LORA + ICLR_LORA + GATE_LORA
D_FF = 5632
LN_EPS = 1e-5
RMS_EPS = 1e-6
GN_EPS = 64e-5
DEPTH = 1
ALPHA = (2.0 * DEPTH) ** 0.25

LANES = 128
QK_PAD = 2 * LANES
MLA_COLS = OFF_RW + QK_ROPE
CHUNK = 64
HEAD_PAIRS = RWKV_HEADS // 2
VMEM_LIMIT = 56 * 1024 * 1024


def _cp(sem):
    return pltpu.CompilerParams(dimension_semantics=sem, vmem_limit_bytes=VMEM_LIMIT)


def _dot(a, b):
    return jnp.dot(a, b, preferred_element_type=F32)


def _dot_nt(a, b):
    return lax.dot_general(a, b, (((1,), (1,)), ((), ())), preferred_element_type=F32)


def _dot_tn(a, b):
    return lax.dot_general(a, b, (((0,), (0,)), ((), ())), preferred_element_type=F32)


def _sigmoid(x):
    return 1.0 / (1.0 + jnp.exp(-x))


def _layer_norm(h, g, b):
    mu = jnp.mean(h, -1, keepdims=True)
    d = h - mu
    var = jnp.mean(d * d, -1, keepdims=True)
    return d * lax.rsqrt(var + LN_EPS) * g + b


def _split3(x):
    hi = x.astype(BF16)
    r1 = x - hi.astype(F32)
    mid = r1.astype(BF16)
    lo = (r1 - mid.astype(F32)).astype(BF16)
    return hi, mid, lo


def _dot3_right(x, m):
    hi, mid, lo = _split3(x)
    return _dot(hi, m) + _dot(mid, m) + _dot(lo, m)


def _dot3_left(m, x):
    hi, mid, lo = _split3(x)
    return _dot(m, hi) + _dot(m, mid) + _dot(m, lo)


def _mod_kernel(c_ref, w_ref, b_ref, o_ref):
    c = c_ref[...]
    s = (c * _sigmoid(c)).astype(BF16)
    o_ref[...] = _dot(s, w_ref[...].astype(BF16)) + b_ref[...]


def _modulation(cond, w_mod, b_mod):
    rows, tn = cond.shape[0], 1024
    n = w_mod.shape[1]
    return pl.pallas_call(
        _mod_kernel,
        grid=(n // tn,),
        in_specs=[pl.BlockSpec((rows, D_MODEL), lambda j: (0, 0)),
                  pl.BlockSpec((D_MODEL, tn), lambda j: (0, j)),
                  pl.BlockSpec((1, tn), lambda j: (0, j))],
        out_specs=pl.BlockSpec((rows, tn), lambda j: (0, j)),
        out_shape=jax.ShapeDtypeStruct((rows, n), F32),
        compiler_params=_cp(("arbitrary",)),
        name="modulation",
    )(cond, w_mod, b_mod.reshape(1, n))


def _mod_spec(which, seq_of):
    return pl.BlockSpec((None, None, 1, D_MODEL), lambda *g: (seq_of(*g), which, 0, 0))


def _in_kernel(x_ref, sh_ref, sc_ref, wm_ref, wr_ref, om_ref, or_ref):
    xm = (x_ref[...] * (1.0 + sc_ref[...]) + sh_ref[...]).astype(BF16)
    om_ref[...] = _dot(xm, wm_ref[...])
    or_ref[...] = _dot(xm, wr_ref[...])


def _in_proj(x2, mod4, seq0, n_seq_tokens, w_mla, w_rw):
    t, tm = x2.shape[0], 256
    per_seq = n_seq_tokens // tm
    seq_of = (lambda i: seq0) if per_seq == 0 else (lambda i: seq0 + i // per_seq)
    return pl.pallas_call(
        _in_kernel,
        grid=(t // tm,),
        in_specs=[pl.BlockSpec((tm, D_MODEL), lambda i: (i, 0)),
                  _mod_spec(0, seq_of), _mod_spec(1, seq_of),
                  pl.BlockSpec((D_MODEL, MLA_COLS), lambda i: (0, 0)),
                  pl.BlockSpec((D_MODEL, RW_COLS), lambda i: (0, 0))],
        out_specs=[pl.BlockSpec((tm, MLA_COLS), lambda i: (i, 0)),
                   pl.BlockSpec((tm, RW_COLS), lambda i: (i, 0))],
        out_shape=[jax.ShapeDtypeStruct((t, MLA_COLS), F32),
                   jax.ShapeDtypeStruct((t, RW_COLS), F32)],
        compiler_params=_cp(("arbitrary",)),
        name="in_proj",
    )(x2, mod4, mod4, w_mla, w_rw)


def _q_kernel(*refs, rope):
    if rope:
        qd_ref, g_ref, w_ref, wsw_ref, c_ref, s_ref, o_ref = refs
    else:
        qd_ref, g_ref, w_ref, o_ref = refs
    x = qd_ref[...]
    qn = (x * lax.rsqrt(jnp.mean(x * x, -1, keepdims=True) + RMS_EPS) * g_ref[...]).astype(BF16)
    q = _dot(qn, w_ref[...])
    if rope:
        c = jnp.concatenate([c_ref[...]] * MLA_HEADS, axis=1)
        s = jnp.concatenate([s_ref[...]] * MLA_HEADS, axis=1)
        q = q * c + _dot(qn, wsw_ref[...]) * s
    o_ref[...] = q.astype(BF16)


def _q_proj(proj_mla, n_seq_tokens, q_norm_g, wq_full, wq_sw, rope_tabs):
    t, tm = proj_mla.shape[0], 256
    per_seq = n_seq_tokens // tm
    rope = rope_tabs is not None
    width = MLA_HEADS * QK_PAD
    const = lambda i: (0, 0)
    in_specs = [pl.BlockSpec((tm, Q_LORA), lambda i: (i, 0)),
                pl.BlockSpec((1, Q_LORA), const),
                pl.BlockSpec((Q_LORA, width), const)]
    args = [proj_mla, q_norm_g.reshape(1, Q_LORA), wq_full]
    if rope:
        in_specs += [pl.BlockSpec((Q_LORA, width), const),
                     pl.BlockSpec((tm, QK_PAD), lambda i: (i % per_seq, 0)),
                     pl.BlockSpec((tm, QK_PAD), lambda i: (i % per_seq, 0))]
        args += [wq_sw, rope_tabs[0], rope_tabs[1]]
    return pl.pallas_call(
        functools.partial(_q_kernel, rope=rope),
        grid=(t // tm,),
        in_specs=in_specs,
        out_specs=pl.BlockSpec((tm, width), lambda i: (i, 0)),
        out_shape=jax.ShapeDtypeStruct((t, width), BF16),
        compiler_params=_cp(("arbitrary",)),
        name="q_proj",
    )(*args)


def _kv_kernel(*refs, norm, rope, emit_new):
    it = iter(refs)
    ckv_ref, kr_ref = next(it), next(it)
    g_ref = next(it) if norm else None
    cs_ref = next(it) if rope else None
    wk_ref, wv_ref, e_ref = next(it), next(it), next(it)
    k_out, v_out = next(it), next(it)
    x = ckv_ref[...]
    if norm:
        x = x * lax.rsqrt(jnp.mean(x * x, -1, keepdims=True) + RMS_EPS) * g_ref[...]
    xb = x.astype(BF16)
    kr = kr_ref[...]
    if rope:
        y = kr * cs_ref[...]
        kr = y + pltpu.roll(y, QK_ROPE, 1)
    kr = kr[:, :QK_ROPE]
    k_out[...] = (_dot(xb, wk_ref[...]) + _dot(kr.astype(BF16), e_ref[...])).astype(BF16)
    v_out[...] = _dot(xb, wv_ref[...]).astype(BF16)
    if emit_new:
        nc_out, nk_out = next(it), next(it)
        nc_out[...] = x
        nk_out[...] = kr


def _kv_proj(ckv_src, ckv_blk, kr_src, kr_blk, kr_w, n_seq_tokens, kv_norm_g, cs_tab, wk_full, wv, e_mat, emit_new):
    t, tm = ckv_src.shape[0], 256
    per_seq = n_seq_tokens // tm
    norm, rope = kv_norm_g is not None, cs_tab is not None
    kw, vw = MLA_HEADS * QK_PAD, MLA_HEADS * V_HEAD
    const = lambda i: (0, 0)
    in_specs = [pl.BlockSpec((tm, KV_LORA), lambda i: (i, ckv_blk)),
                pl.BlockSpec((tm, kr_w), lambda i: (i, kr_blk))]
    args = [ckv_src, kr_src]
    if norm:
        in_specs.append(pl.BlockSpec((1, KV_LORA), const))
        args.append(kv_norm_g.reshape(1, KV_LORA))
    if rope:
        in_specs.append(pl.BlockSpec((tm, LANES), lambda i: (i % per_seq, 0)))
        args.append(cs_tab)
    in_specs += [pl.BlockSpec((KV_LORA, kw), const), pl.BlockSpec((KV_LORA, vw), const),
                 pl.BlockSpec((QK_ROPE, kw), const)]
    args += [wk_full, wv, e_mat]
    out_specs = [pl.BlockSpec((tm, kw), lambda i: (i, 0)), pl.BlockSpec((tm, vw), lambda i: (i, 0))]
    out_shape = [jax.ShapeDtypeStruct((t, kw), BF16), jax.ShapeDtypeStruct((t, vw), BF16)]
    if emit_new:
        out_specs += [pl.BlockSpec((tm, KV_LORA), lambda i: (i, 0)), pl.BlockSpec((tm, QK_ROPE), lambda i: (i, 0))]
        out_shape += [jax.ShapeDtypeStruct((t, KV_LORA), F32), jax.ShapeDtypeStruct((t, QK_ROPE), F32)]
    return pl.pallas_call(
        functools.partial(_kv_kernel, norm=norm, rope=rope, emit_new=emit_new),
        grid=(t // tm,),
        in_specs=in_specs, out_specs=out_specs, out_shape=out_shape,
        compiler_params=_cp(("arbitrary",)),
        name="kv_proj",
    )(*args)


def _attn_kernel(*refs, cache, scale):
    if cache:
        q_ref, k_ref, v_ref, kc_ref, vc_ref, o_ref = refs
    else:
        q_ref, k_ref, v_ref, o_ref = refs
    q = q_ref[...]
    s = _dot_nt(q, k_ref[...]) * scale
    m = jnp.max(s, -1, keepdims=True)
    if cache:
        sc = _dot_nt(q, kc_ref[...]) * scale
        m = jnp.maximum(m, jnp.max(sc, -1, keepdims=True))
    p = jnp.exp(s - m)
    l = jnp.sum(p, -1, keepdims=True)
    if cache:
        pc = jnp.exp(sc - m)
        l = l + jnp.sum(pc, -1, keepdims=True)
    inv = 1.0 / l
    acc = _dot((p * inv).astype(BF16), v_ref[...])
    if cache:
        acc = acc + _dot((pc * inv).astype(BF16), vc_ref[...])
    o_ref[...] = acc.astype(BF16)


def _attention(q, k, v, kc, vc, batch, n_tok):
    tq = 256
    n_q = n_tok // tq
    cache = kc is not None
    in_specs = [pl.BlockSpec((tq, QK_PAD), lambda b, h, i: (b * n_q + i, h)),
                pl.BlockSpec((n_tok, QK_PAD), lambda b, h, i: (b, h)),
                pl.BlockSpec((n_tok, V_HEAD), lambda b, h, i: (b, h))]
    args = [q, k, v]
    if cache:
        n_c = kc.shape[0] // batch
        in_specs += [pl.BlockSpec((n_c, QK_PAD), lambda b, h, i: (b, h)),
                     pl.BlockSpec((n_c, V_HEAD), lambda b, h, i: (b, h))]
        args += [kc, vc]
    return pl.pallas_call(
        functools.partial(_attn_kernel, cache=cache, scale=1.0 / math.sqrt(QK_NOPE + QK_ROPE)),
        grid=(batch, MLA_HEADS, n_q),
        in_specs=in_specs,
        out_specs=pl.BlockSpec((tq, V_HEAD), lambda b, h, i: (b * n_q + i, h)),
        out_shape=jax.ShapeDtypeStruct((batch * n_tok, MLA_HEADS * V_HEAD), BF16),
        compiler_params=_cp(("arbitrary", "arbitrary", "arbitrary")),
        name="attention",
    )(*args)


def _pair_blockdiag(y, lane_lo):
    zero = jnp.zeros_like(y)
    return jnp.concatenate([jnp.where(lane_lo, y, zero), jnp.where(lane_lo, zero, y)], axis=0)


def _scan_kernel(*refs, tm, reverse, zero_init, emit_state, finalize):
    it = iter(refs)
    blocks = {}
    cols = ("r", "k", "v", "wa") + (("gd",) if finalize else ())
    for name in cols:
        blocks[name] = (next(it), next(it), next(it), next(it))
    kk_ref, ka_ref, w0_ref, wup_ref, a0_ref, aup_ref = (next(it) for _ in range(6))
    if finalize:
        rk_ref, a0o_ref, aupo_ref, gup_ref, gng_ref, gnb_ref, yo_ref = (next(it) for _ in range(7))
    ones_ref, tri_ref = next(it), next(it)
    s0_ref = None if zero_init else next(it)
    y_ref = next(it)
    sout_ref = next(it) if emit_state else None
    s_scr = next(it)

    t_idx = pl.program_id(2)
    n_t = pl.num_programs(2)
    tile = (n_t - 1 - t_idx) if reverse else t_idx
    first_tok = tile == 0
    last_tok = tile == n_t - 1

    @pl.when(t_idx == 0)
    def _():
        if zero_init:
            s_scr[...] = jnp.zeros_like(s_scr)
        else:
            s_scr[...] = s0_ref[...]

    row = lax.broadcasted_iota(jnp.int32, (tm, LANES), 0)

    def shifted(name):
        main, prv, nxt, mu = blocks[name]
        u = main[...]
        p_row = jnp.where(first_tok, 0.0, prv[7:8, :])
        n_row = jnp.where(last_tok, 0.0, nxt[0:1, :])
        up = jnp.where(row == 0, p_row, pltpu.roll(u, 1, 0))
        un = jnp.where(row == tm - 1, n_row, pltpu.roll(u, tm - 1, 0))
        return u + mu[...] * (0.5 * (up + un) - u)

    ones_bd = ones_ref[...]
    seg = lambda x: _dot3_right(x, ones_bd)

    r, k, v, wa = shifted("r"), shifted("k"), shifted("v"), shifted("wa")
    wab = wa.astype(BF16)
    kk = k * kk_ref[...]
    kk = kk * lax.rsqrt(jnp.maximum(seg(kk * kk), 1e-24))
    ld = _sigmoid(w0_ref[...] + _dot(jnp.tanh(wa).astype(BF16), wup_ref[...])) * (-math.exp(-0.5))
    a = _sigmoid(a0_ref[...] + _dot(wab, aup_ref[...]))
    kd = k * (1.0 + (a - 1.0) * ka_ref[...])
    b = kk * a

    c_row = lax.broadcasted_iota(jnp.int32, (CHUNK, LANES), 0)
    c_lane = lax.broadcasted_iota(jnp.int32, (CHUNK, LANES), 1)
    c_src = c_lane & (CHUNK - 1)
    lane_lo = c_lane < CHUNK
    strict = (c_src > c_row) if reverse else (c_src < c_row)
    incl = (c_src >= c_row) if reverse else (c_src <= c_row)
    eye = (c_src == c_row).astype(F32)
    bd_row = lax.broadcasted_iota(jnp.int32, (LANES, LANES), 0)
    bd_lane = lax.broadcasted_iota(jnp.int32, (LANES, LANES), 1)
    bd_mask = jnp.right_shift(bd_row, 6) == jnp.right_shift(bd_lane, 6)
    tri = tri_ref[...]

    def nn(p, y):
        return _dot(p.astype(BF16), _pair_blockdiag(y, lane_lo).astype(BF16))

    s_bd = s_scr[...]
    n_chunks = tm // CHUNK
    order = range(n_chunks - 1, -1, -1) if reverse else range(n_chunks)
    y_chunks = [None] * n_chunks
    for c in order:
        sl = slice(c * CHUNK, (c + 1) * CHUNK)
        r_c, v_c, kk_c, kd_c, b_c, ld_c = r[sl], v[sl], kk[sl], kd[sl], b[sl], ld[sl]
        cum = _dot3_left(tri, ld_c)
        tot = cum[0:1] if reverse else cum[CHUNK - 1:CHUNK]
        e_neg = jnp.exp(-cum)
        e_tot = jnp.exp(tot - cum)
        a_hat = -kk_c * jnp.exp(cum - ld_c)
        r_hat = r_c * jnp.exp(cum)
        ar = jnp.concatenate([a_hat, r_hat], axis=0).astype(BF16)
        gb = _dot_nt(ar, _pair_blockdiag(b_c * e_neg, lane_lo).astype(BF16))
        gk = _dot_nt(ar, _pair_blockdiag(kd_c * e_neg, lane_lo).astype(BF16))
        lmat = jnp.where(strict, gb[:CHUNK], 0.0)
        akm = jnp.where(strict, gk[:CHUNK], 0.0)
        rbm = jnp.where(incl, gb[CHUNK:], 0.0)
        rkm = jnp.where(incl, gk[CHUNK:], 0.0)
        pmat, qmat = eye + lmat, lmat
        for _ in range(int(math.log2(CHUNK)) - 1):
            qmat = nn(qmat, qmat)
            pmat = pmat + nn(pmat, qmat)
        xs = _dot_nt(ar, s_bd.astype(BF16))
        u = nn(pmat, xs[:CHUNK] + nn(akm, v_c))
        y_chunks[c] = xs[CHUNK:] + _dot(
            jnp.concatenate([rbm, rkm], axis=1).astype(BF16),
            jnp.concatenate([_pair_blockdiag(u, lane_lo), _pair_blockdiag(v_c, lane_lo)], axis=0).astype(BF16))
        upd = _dot_tn(jnp.concatenate([u, v_c], axis=0).astype(BF16),
                      jnp.concatenate([b_c * e_tot, kd_c * e_tot], axis=0).astype(BF16))
        s_bd = s_bd * jnp.exp(tot) + jnp.where(bd_mask, upd, 0.0)
    s_scr[...] = s_bd
    y = jnp.concatenate(y_chunks, axis=0)

    if emit_state:
        @pl.when(t_idx == n_t - 1)
        def _():
            sout_ref[...] = s_bd

    if not finalize:
        y_ref[...] = y
    else:
        rk = rk_ref[...]
        a_o = _sigmoid(a0o_ref[...] + _dot(wab, aupo_ref[...]))
        kd_o = k * (1.0 + (a_o - 1.0) * ka_ref[...])
        bonus = seg(r * kd * rk) + seg(r * kd_o * rk)
        gate = _dot(_sigmoid(shifted("gd")).astype(BF16), gup_ref[...])
        yt = y + yo_ref[...]
        inv_n = 1.0 / RWKV_HEAD
        d = yt - seg(yt) * inv_n
        var = seg(d * d) * inv_n
        yn = d * lax.rsqrt(var + GN_EPS) * gng_ref[...] + gnb_ref[...]
        y_ref[...] = ((yn + bonus * v) * gate).astype(BF16)


def _scan(proj_rw, batch, n_tok, prm, direction, s0_bd, emit_state, y_other):
    reverse = direction == 1
    finalize = y_other is not None
    zero_init = s0_bd is None
    t = batch * n_tok
    tm = 256
    n_t = n_tok // tm
    row8 = tm // 8
    n_row8 = t // 8

    def tile_of(ti):
        return (n_t - 1 - ti) if reverse else ti

    main = lambda cb: pl.BlockSpec((tm, LANES), lambda bb, p, ti, cb=cb: (bb * n_t + tile_of(ti), cb(p)))
    prv = lambda cb: pl.BlockSpec(
        (8, LANES), lambda bb, p, ti, cb=cb: (jnp.maximum((bb * n_t + tile_of(ti)) * row8 - 1, 0), cb(p)))
    nxt = lambda cb: pl.BlockSpec(
        (8, LANES), lambda bb, p, ti, cb=cb: (jnp.minimum((bb * n_t + tile_of(ti) + 1) * row8, n_row8 - 1), cb(p)))
    rowvec = lambda cb: pl.BlockSpec((1, LANES), lambda bb, p, ti, cb=cb: (0, cb(p)))
    lora = lambda: pl.BlockSpec((LANES, LANES), lambda bb, p, ti: (0, p))
    const = lambda shape: pl.BlockSpec(shape, lambda bb, p, ti: (0, 0))
    state = lambda: pl.BlockSpec((None, None, LANES, LANES), lambda bb, p, ti: (bb, p, 0, 0))

    col_blocks = {"r": lambda p: p, "k": lambda p: HEAD_PAIRS + p, "v": lambda p: 2 * HEAD_PAIRS + p,
                  "wa": lambda p: 3 * HEAD_PAIRS, "gd": lambda p: 3 * HEAD_PAIRS + 1}
    in_specs, args = [], []
    for name in ("r", "k", "v", "wa") + (("gd",) if finalize else ()):
        cb = col_blocks[name]
        in_specs += [main(cb), prv(cb), nxt(cb), rowvec(cb)]
        args += [proj_rw, proj_rw, proj_rw, prm["mu"]]
    d, o = prm["dirs"][direction], prm["dirs"][1 - direction]
    pair = lambda p: p
    in_specs += [rowvec(pair), rowvec(pair), rowvec(pair), lora(), rowvec(pair), lora()]
    args += [prm["k_k"], prm["k_a"], d["w0"], d["w_up"], d["a0"], d["a_up"]]
    if finalize:
        in_specs += [rowvec(pair), rowvec(pair), lora(), lora(), rowvec(pair), rowvec(pair),
                     pl.BlockSpec((tm, LANES), lambda bb, p, ti: (bb * n_t + tile_of(ti), p))]
        args += [prm["r_k"], o["a0"], o["a_up"], prm["g_up"], prm["gn_g"], prm["gn_b"], y_other]
    in_specs += [const((LANES, LANES)), const((CHUNK, CHUNK))]
    args += [prm["ones_bd"], prm["tri"][direction]]
    if not zero_init:
        in_specs.append(state())
        args.append(s0_bd)
    out_specs = [pl.BlockSpec((tm, LANES), lambda bb, p, ti: (bb * n_t + tile_of(ti), p))]
    out_shape = [jax.ShapeDtypeStruct((t, RWKV_WIDTH), BF16 if finalize else F32)]
    if emit_state:
        out_specs.append(state())
        out_shape.append(jax.ShapeDtypeStruct((batch, HEAD_PAIRS, LANES, LANES), F32))
    out = pl.pallas_call(
        functools.partial(_scan_kernel, tm=tm, reverse=reverse, zero_init=zero_init,
                          emit_state=emit_state, finalize=finalize),
        grid=(batch, HEAD_PAIRS, n_t),
        in_specs=in_specs, out_specs=out_specs, out_shape=out_shape,
        scratch_shapes=[pltpu.VMEM((LANES, LANES), F32)],
        compiler_params=_cp(("arbitrary", "arbitrary", "arbitrary")),
        name="rwkv_scan_bwd" if reverse else "rwkv_scan_fwd",
    )(*args)
    return out if emit_state else (out[0], None)


def _out_kernel(att_ref, rw_ref, x_ref, g_ref, wa_ref, wr_ref, lg_ref, lb_ref, o_ref):
    f = _dot(att_ref[...], wa_ref[...]) + _dot(rw_ref[...], wr_ref[...])
    o_ref[...] = _layer_norm(ALPHA * x_ref[...] + g_ref[...] * f, lg_ref[...], lb_ref[...])


def _out_proj(att, rw, x2, mod4, seq0, n_seq_tokens, w_att, w_rw, ln_g, ln_b):
    t, tm = x2.shape[0], 256
    per_seq = n_seq_tokens // tm
    seq_of = lambda i: seq0 + i // per_seq
    half = D_MODEL // 2
    const = lambda i: (0, 0)
    return pl.pallas_call(
        _out_kernel,
        grid=(t // tm,),
        in_specs=[pl.BlockSpec((tm, half), lambda i: (i, 0)), pl.BlockSpec((tm, half), lambda i: (i, 0)),
                  pl.BlockSpec((tm, D_MODEL), lambda i: (i, 0)), _mod_spec(2, seq_of),
                  pl.BlockSpec((half, D_MODEL), const), pl.BlockSpec((half, D_MODEL), const),
                  pl.BlockSpec((1, D_MODEL), const), pl.BlockSpec((1, D_MODEL), const)],
        out_specs=pl.BlockSpec((tm, D_MODEL), lambda i: (i, 0)),
        out_shape=jax.ShapeDtypeStruct((t, D_MODEL), F32),
        compiler_params=_cp(("arbitrary",)),
        name="out_proj",
    )(att, rw, x2, mod4, w_att, w_rw, ln_g.reshape(1, D_MODEL), ln_b.reshape(1, D_MODEL))


def _ffn_kernel(x_ref, sh_ref, sc_ref, g_ref, wg_ref, wu_ref, wd_ref, lg_ref, lb_ref, o_ref, h_scr, acc_scr):
    kf = pl.program_id(1)

    @pl.when(kf == 0)
    def _():
        h_scr[...] = (x_ref[...] * (1.0 + sc_ref[...]) + sh_ref[...]).astype(BF16)
        acc_scr[...] = jnp.zeros_like(acc_scr)

    h = h_scr[...]
    gate = _dot(h, wg_ref[...])
    act = (gate * _sigmoid(gate) * _dot(h, wu_ref[...])).astype(BF16)
    acc_scr[...] += _dot(act, wd_ref[...])

    @pl.when(kf == pl.num_programs(1) - 1)
    def _():
        o_ref[...] = _layer_norm(ALPHA * x_ref[...] + g_ref[...] * acc_scr[...], lg_ref[...], lb_ref[...])


def _ffn(x1, mod4, seq0, n_seq_tokens, w_gate, w_up, w_down, ln_g, ln_b):
    t, tm, tf = x1.shape[0], 512, 512
    tm = min(tm, n_seq_tokens)
    per_seq = n_seq_tokens // tm
    seq_of = lambda i, kf: seq0 + i // per_seq
    const = lambda i, kf: (0, 0)
    return pl.pallas_call(
        _ffn_kernel,
        grid=(t // tm, D_FF // tf),
        in_specs=[pl.BlockSpec((tm, D_MODEL), lambda i, kf: (i, 0)),
                  _mod_spec(3, seq_of), _mod_spec(4, seq_of), _mod_spec(5, seq_of),
                  pl.BlockSpec((D_MODEL, tf), lambda i, kf: (0, kf)),
                  pl.BlockSpec((D_MODEL, tf), lambda i, kf: (0, kf)),
                  pl.BlockSpec((tf, D_MODEL), lambda i, kf: (kf, 0)),
                  pl.BlockSpec((1, D_MODEL), const), pl.BlockSpec((1, D_MODEL), const)],
        out_specs=pl.BlockSpec((tm, D_MODEL), lambda i, kf: (i, 0)),
        out_shape=jax.ShapeDtypeStruct((t, D_MODEL), F32),
        scratch_shapes=[pltpu.VMEM((tm, D_MODEL), BF16), pltpu.VMEM((tm, D_MODEL), F32)],
        compiler_params=_cp(("arbitrary", "arbitrary")),
        name="ffn",
    )(x1, mod4, mod4, mod4, w_gate, w_up, w_down, ln_g.reshape(1, D_MODEL), ln_b.reshape(1, D_MODEL))


def _rope_tables(n_tokens):
    rows = n_tokens // GRID_W
    row = jnp.repeat(jnp.arange(rows), GRID_W).astype(F32)
    col = jnp.tile(jnp.arange(GRID_W), rows).astype(F32)
    freqs = ROPE_THETA ** (-jnp.arange(ROPE_AXIS_FREQS, dtype=F32) / ROPE_AXIS_FREQS)
    ang = jnp.concatenate([row[:, None] * freqs, col[:, None] * freqs], -1)
    cos, sin = jnp.cos(ang), jnp.sin(ang)
    cos2 = jnp.concatenate([cos, cos], -1)
    sin2 = jnp.concatenate([-sin, sin], -1)
    pad_lo = jnp.zeros((n_tokens, QK_NOPE), F32)
    pad_hi = jnp.zeros((n_tokens, QK_PAD - QK_NOPE - QK_ROPE), F32)
    q_cos = jnp.concatenate([pad_lo + 1.0, cos2, pad_hi], -1)
    q_sin = jnp.concatenate([pad_lo, sin2, pad_hi], -1)
    k_tab = jnp.concatenate([cos2, sin2], -1)
    return q_cos, q_sin, k_tab


def _swap_halves(w):
    half = w.shape[-1] // 2
    return jnp.concatenate([w[..., half:], w[..., :half]], -1)


def _block_diag_state(s):
    bsz = s.shape[0]
    s = s.reshape(bsz, HEAD_PAIRS, 2, RWKV_HEAD, RWKV_HEAD)
    z = jnp.zeros_like(s[:, :, 0])
    top = jnp.concatenate([s[:, :, 0], z], -1)
    bot = jnp.concatenate([z, s[:, :, 1]], -1)
    return jnp.concatenate([top, bot], -2)


def _unblock_diag_state(s_bd):
    bsz = s_bd.shape[0]
    s = jnp.stack([s_bd[:, :, :RWKV_HEAD, :RWKV_HEAD], s_bd[:, :, RWKV_HEAD:, RWKV_HEAD:]], 2)
    return s.reshape(bsz, RWKV_HEADS, RWKV_HEAD, RWKV_HEAD)


def _layer(x2, batch, n_tok, mod4, seq0, per_seq_mod, wts, rope_tabs, cache, states, is_context):
    mod_tokens = n_tok if per_seq_mod else 0
    proj_mla, proj_rw = _in_proj(x2, mod4, seq0, mod_tokens, wts["w_mla"], wts["w_rw"])
    q = _q_proj(proj_mla, n_tok, wts["q_norm_g"], wts["wq_full"], wts["wq_sw"],
                None if rope_tabs is None else rope_tabs[:2])
    kv = _kv_proj(proj_mla, OFF_KV // KV_LORA, proj_mla, OFF_KR // LANES, LANES, n_tok, wts["kv_norm_g"],
                  None if rope_tabs is None else rope_tabs[2], wts["wk_full"], wts["w_uv"], wts["e_mat"], is_context)
    kc = vc = None
    if cache is not None:
        kc, vc = _kv_proj(cache[0], 0, cache[1], 0, QK_ROPE, cache[0].shape[0] // batch, None, None,
                          wts["wk_full"], wts["w_uv"], wts["e_mat"], False)
    att = _attention(q, kv[0], kv[1], kc, vc, batch, n_tok)
    s_f = None if states is None else _block_diag_state(states[0])
    s_b = None if states is None else _block_diag_state(states[1])
    y_f, new_f = _scan(proj_rw, batch, n_tok, wts["rwkv"], 0, s_f, is_context, None)
    rw, new_b = _scan(proj_rw, batch, n_tok, wts["rwkv"], 1, s_b, is_context, y_f)
    seq_tokens = n_tok if per_seq_mod else x2.shape[0]
    x1 = _out_proj(att, rw, x2, mod4, seq0, seq_tokens, wts["w_out_att"], wts["w_out_rw"], wts["ln1_g"], wts["ln1_b"])
    y = _ffn(x1, mod4, seq0, seq_tokens, wts["w_ffn_gate"], wts["w_ffn_up"], wts["w_ffn_down"],
             wts["ln2_g"], wts["ln2_b"])
    return y, kv, new_f, new_b


def kernel(x_prompt, x_sample, cache_ckv, cache_krope, state_wkv_fwd, state_wkv_bwd, c, c_ctx, w_mod, b_mod, w_in, q_norm_g, kv_norm_g, w_uq, w_uk, w_uv, tok_shift_mu, w0_fwd, w_up_fwd, a0_fwd, a_up_fwd, w0_bwd, w_up_bwd, a0_bwd, a_up_bwd, g_up, k_k, k_a, r_k, gn_g, gn_b, w_out, ln1_g, ln1_b, w_ffn_gate, w_ffn_up, w_ffn_down, ln2_g, ln2_b):
    batch, seq = x_prompt.shape[:2]
    dec_batch, dec_seq = x_sample.shape[:2]
    layer = 0

    mod_rows = 16
    cond = jnp.concatenate([c, c_ctx[None, :], jnp.zeros((mod_rows - dec_batch - 1, D_MODEL), F32)], 0)
    mod4 = _modulation(cond, w_mod[layer], b_mod[layer]).reshape(mod_rows, 6, 1, D_MODEL)

    wi = w_in[layer]
    kr_w = wi[:, OFF_KR:OFF_RW]
    uq = w_uq[layer].reshape(Q_LORA, MLA_HEADS, QK_NOPE + QK_ROPE)
    uq_pad = jnp.zeros((Q_LORA, MLA_HEADS, QK_PAD - QK_NOPE - QK_ROPE), F32)
    wq_full = jnp.concatenate([uq, uq_pad], -1).reshape(Q_LORA, MLA_HEADS * QK_PAD)
    wq_sw = jnp.concatenate([jnp.zeros((Q_LORA, MLA_HEADS, QK_NOPE), F32), _swap_halves(uq[..., QK_NOPE:]), uq_pad],
                            -1).reshape(Q_LORA, MLA_HEADS * QK_PAD)
    uk = w_uk[layer].reshape(KV_LORA, MLA_HEADS, QK_NOPE)
    wk_full = jnp.concatenate([uk, jnp.zeros((KV_LORA, MLA_HEADS, QK_PAD - QK_NOPE), F32)], -1)
    e_head = jnp.concatenate([jnp.zeros((QK_ROPE, QK_NOPE), F32), jnp.eye(QK_ROPE, dtype=F32),
                              jnp.zeros((QK_ROPE, QK_PAD - QK_NOPE - QK_ROPE), F32)], -1)
    lora_pad = jnp.zeros((DECAY_LORA, RWKV_WIDTH), F32)
    row = lambda vec: vec[layer].reshape(1, -1)
    head_of = jnp.arange(LANES) // RWKV_HEAD
    tri_lo = jnp.tril(jnp.ones((CHUNK, CHUNK), F32))
    rwkv = {
        "mu": row(tok_shift_mu), "k_k": row(k_k), "k_a": row(k_a), "r_k": row(r_k),
        "gn_g": row(gn_g), "gn_b": row(gn_b), "g_up": g_up[layer].astype(BF16),
        "dirs": [
            {"w0": row(w0_fwd), "a0": row(a0_fwd),
             "w_up": jnp.concatenate([w_up_fwd[layer], lora_pad], 0).astype(BF16),
             "a_up": jnp.concatenate([lora_pad, a_up_fwd[layer]], 0).astype(BF16)},
            {"w0": row(w0_bwd), "a0": row(a0_bwd),
             "w_up": jnp.concatenate([w_up_bwd[layer], lora_pad], 0).astype(BF16),
             "a_up": jnp.concatenate([lora_pad, a_up_bwd[layer]], 0).astype(BF16)}],
        "ones_bd": (head_of[:, None] == head_of[None, :]).astype(BF16),
        "tri": [tri_lo.astype(BF16), tri_lo.T.astype(BF16)],
    }
    wts = {
        "w_mla": jnp.concatenate([wi[:, :OFF_RW], _swap_halves(kr_w)], -1).astype(BF16),
        "w_rw": wi[:, OFF_RW:].astype(BF16),
        "q_norm_g": q_norm_g[layer], "kv_norm_g": kv_norm_g[layer],
        "wq_full": wq_full.astype(BF16), "wq_sw": wq_sw.astype(BF16),
        "wk_full": wk_full.reshape(KV_LORA, MLA_HEADS * QK_PAD).astype(BF16),
        "w_uv": w_uv[layer].astype(BF16),
        "e_mat": jnp.tile(e_head, (1, MLA_HEADS)).astype(BF16),
        "rwkv": rwkv,
        "w_out_att": w_out[layer][:MLA_HEADS * V_HEAD].astype(BF16),
        "w_out_rw": w_out[layer][MLA_HEADS * V_HEAD:].astype(BF16),
        "ln1_g": ln1_g[layer], "ln1_b": ln1_b[layer],
        "w_ffn_gate": w_ffn_gate[layer].astype(BF16), "w_ffn_up": w_ffn_up[layer].astype(BF16),
        "w_ffn_down": w_ffn_down[layer].astype(BF16),
        "ln2_g": ln2_g[layer], "ln2_b": ln2_b[layer],
    }

    y_prompt, kv_ctx, s_f, s_b = _layer(
        x_prompt.reshape(batch * seq, D_MODEL), batch, seq, mod4, dec_batch, False, wts, None, None, None, True)
    past = cache_ckv.shape[2]
    cache = (cache_ckv[:, layer].reshape(dec_batch * past, KV_LORA),
             cache_krope[:, layer].reshape(dec_batch * past, QK_ROPE))
    y_sample, _, _, _ = _layer(
        x_sample.reshape(dec_batch * dec_seq, D_MODEL), dec_batch, dec_seq, mod4, 0, True, wts,
        _rope_tables(dec_seq), cache, (state_wkv_fwd[:, layer], state_wkv_bwd[:, layer]), False)

    return (y_prompt.reshape(batch, seq, D_MODEL),
            y_sample.reshape(dec_batch, dec_seq, D_MODEL),
            kv_ctx[2].reshape(batch, 1, seq, KV_LORA),
            kv_ctx[3].reshape(batch, 1, seq, QK_ROPE),
            _unblock_diag_state(s_f)[:, None],
            _unblock_diag_state(s_b)[:, None])
```

```python
import functools
import math

import jax
import jax.numpy as jnp
from jax import lax
from jax.experimental import pallas as pl
from jax.experimental.pallas import tpu as pltpu

F32 = jnp.float32
BF16 = jnp.bfloat16

D_MODEL = 2048
GRID_W = 64
MLA_HEADS = 8
QK_NOPE = 128
QK_ROPE = 64
V_HEAD = 128
Q_LORA = 512
KV_LORA = 256
ROPE_AXIS_FREQS = QK_ROPE // 4
ROPE_THETA = 10000.0
RWKV_HEADS = 16
RWKV_HEAD = 64
RWKV_WIDTH = RWKV_HEADS * RWKV_HEAD
DECAY_LORA = 64
ICLR_LORA = 64
GATE_LORA = 128
OFF_KV = Q_LORA
OFF_KR = OFF_KV + KV_LORA
OFF_RW = OFF_KR + QK_ROPE
RW_COLS = 3 * RWKV_WIDTH + DECAY_LORA + ICLR_LORA + GATE_LORA
D_FF = 5632
LN_EPS = 1e-5
RMS_EPS = 1e-6
GN_EPS = 64e-5
DEPTH = 1
ALPHA = (2.0 * DEPTH) ** 0.25

LANES = 128
QK_PAD = 2 * LANES
MLA_COLS = OFF_RW + QK_ROPE
CHUNK = 64
HEAD_PAIRS = RWKV_HEADS // 2
SCAN_PAIRS = 8
VMEM_LIMIT = 56 * 1024 * 1024


def _cp(sem):
    return pltpu.CompilerParams(dimension_semantics=sem, vmem_limit_bytes=VMEM_LIMIT)


def _dot(a, b):
    return jnp.dot(a, b, preferred_element_type=F32)


def _dot_nt(a, b):
    return lax.dot_general(a, b, (((1,), (1,)), ((), ())), preferred_element_type=F32)


def _dot_tn(a, b):
    return lax.dot_general(a, b, (((0,), (0,)), ((), ())), preferred_element_type=F32)


def _sigmoid(x):
    return 1.0 / (1.0 + jnp.exp(-x))


def _layer_norm(h, g, b):
    mu = jnp.mean(h, -1, keepdims=True)
    d = h - mu
    var = jnp.mean(d * d, -1, keepdims=True)
    return d * lax.rsqrt(var + LN_EPS) * g + b


def _split3(x):
    hi = x.astype(BF16)
    r1 = x - hi.astype(F32)
    mid = r1.astype(BF16)
    lo = (r1 - mid.astype(F32)).astype(BF16)
    return hi, mid, lo


def _dot3_right(x, m):
    hi, mid, lo = _split3(x)
    return _dot(hi, m) + _dot(mid, m) + _dot(lo, m)


def _dot3_left(m, x):
    hi, mid, lo = _split3(x)
    return _dot(m, hi) + _dot(m, mid) + _dot(m, lo)


def _mod_kernel(c_ref, w_ref, b_ref, o_ref):
    c = c_ref[...]
    s = (c * _sigmoid(c)).astype(BF16)
    o_ref[...] = _dot(s, w_ref[...].astype(BF16)) + b_ref[...]


def _modulation(cond, w_mod, b_mod):
    rows, tn = cond.shape[0], 1024
    n = w_mod.shape[1]
    return pl.pallas_call(
        _mod_kernel,
        grid=(n // tn,),
        in_specs=[pl.BlockSpec((rows, D_MODEL), lambda j: (0, 0)),
                  pl.BlockSpec((D_MODEL, tn), lambda j: (0, j)),
                  pl.BlockSpec((1, tn), lambda j: (0, j))],
        out_specs=pl.BlockSpec((rows, tn), lambda j: (0, j)),
        out_shape=jax.ShapeDtypeStruct((rows, n), F32),
        compiler_params=_cp(("arbitrary",)),
        name="modulation",
    )(cond, w_mod, b_mod.reshape(1, n))


def _mod_spec(which, seq_of):
    return pl.BlockSpec((None, None, 1, D_MODEL), lambda *g: (seq_of(*g), which, 0, 0))


def _in_kernel(x_ref, sh_ref, sc_ref, wm_ref, wr_ref, om_ref, or_ref):
    xm = (x_ref[...] * (1.0 + sc_ref[...]) + sh_ref[...]).astype(BF16)
    om_ref[...] = _dot(xm, wm_ref[...])
    or_ref[...] = _dot(xm, wr_ref[...])


def _in_proj(x2, mod4, seq0, n_seq_tokens, w_mla, w_rw):
    t, tm = x2.shape[0], 256
    per_seq = n_seq_tokens // tm
    seq_of = (lambda i: seq0) if per_seq == 0 else (lambda i: seq0 + i // per_seq)
    return pl.pallas_call(
        _in_kernel,
        grid=(t // tm,),
        in_specs=[pl.BlockSpec((tm, D_MODEL), lambda i: (i, 0)),
                  _mod_spec(0, seq_of), _mod_spec(1, seq_of),
                  pl.BlockSpec((D_MODEL, MLA_COLS), lambda i: (0, 0)),
                  pl.BlockSpec((D_MODEL, RW_COLS), lambda i: (0, 0))],
        out_specs=[pl.BlockSpec((tm, MLA_COLS), lambda i: (i, 0)),
                   pl.BlockSpec((tm, RW_COLS), lambda i: (i, 0))],
        out_shape=[jax.ShapeDtypeStruct((t, MLA_COLS), F32),
                   jax.ShapeDtypeStruct((t, RW_COLS), F32)],
        compiler_params=_cp(("arbitrary",)),
        name="in_proj",
    )(x2, mod4, mod4, w_mla, w_rw)


def _q_kernel(*refs, rope):
    if rope:
        qd_ref, g_ref, w_ref, wsw_ref, c_ref, s_ref, o_ref = refs
    else:
        qd_ref, g_ref, w_ref, o_ref = refs
    x = qd_ref[...]
    qn = (x * lax.rsqrt(jnp.mean(x * x, -1, keepdims=True) + RMS_EPS) * g_ref[...]).astype(BF16)
    q = _dot(qn, w_ref[...])
    if rope:
        c = jnp.concatenate([c_ref[...]] * MLA_HEADS, axis=1)
        s = jnp.concatenate([s_ref[...]] * MLA_HEADS, axis=1)
        q = q * c + _dot(qn, wsw_ref[...]) * s
    o_ref[...] = q.astype(BF16)


def _q_proj(proj_mla, n_seq_tokens, q_norm_g, wq_full, wq_sw, rope_tabs):
    t, tm = proj_mla.shape[0], 256
    per_seq = n_seq_tokens // tm
    rope = rope_tabs is not None
    width = MLA_HEADS * QK_PAD
    const = lambda i: (0, 0)
    in_specs = [pl.BlockSpec((tm, Q_LORA), lambda i: (i, 0)),
                pl.BlockSpec((1, Q_LORA), const),
                pl.BlockSpec((Q_LORA, width), const)]
    args = [proj_mla, q_norm_g.reshape(1, Q_LORA), wq_full]
    if rope:
        in_specs += [pl.BlockSpec((Q_LORA, width), const),
                     pl.BlockSpec((tm, QK_PAD), lambda i: (i % per_seq, 0)),
                     pl.BlockSpec((tm, QK_PAD), lambda i: (i % per_seq, 0))]
        args += [wq_sw, rope_tabs[0], rope_tabs[1]]
    return pl.pallas_call(
        functools.partial(_q_kernel, rope=rope),
        grid=(t // tm,),
        in_specs=in_specs,
        out_specs=pl.BlockSpec((tm, width), lambda i: (i, 0)),
        out_shape=jax.ShapeDtypeStruct((t, width), BF16),
        compiler_params=_cp(("arbitrary",)),
        name="q_proj",
    )(*args)


def _kv_kernel(*refs, norm, rope, emit_new):
    it = iter(refs)
    ckv_ref, kr_ref = next(it), next(it)
    g_ref = next(it) if norm else None
    cs_ref = next(it) if rope else None
    wk_ref, wv_ref, e_ref = next(it), next(it), next(it)
    k_out, v_out = next(it), next(it)
    x = ckv_ref[...]
    if norm:
        x = x * lax.rsqrt(jnp.mean(x * x, -1, keepdims=True) + RMS_EPS) * g_ref[...]
    xb = x.astype(BF16)
    kr = kr_ref[...]
    if rope:
        y = kr * cs_ref[...]
        kr = y + pltpu.roll(y, QK_ROPE, 1)
    kr = kr[:, :QK_ROPE]
    k_out[...] = (_dot(xb, wk_ref[...]) + _dot(kr.astype(BF16), e_ref[...])).astype(BF16)
    v_out[...] = _dot(xb, wv_ref[...]).astype(BF16)
    if emit_new:
        nc_out, nk_out = next(it), next(it)
        nc_out[...] = x
        nk_out[...] = kr


def _kv_proj(ckv_src, ckv_blk, kr_src, kr_blk, kr_w, n_seq_tokens, kv_norm_g, cs_tab, wk_full, wv, e_mat, emit_new):
    t, tm = ckv_src.shape[0], 256
    per_seq = n_seq_tokens // tm
    norm, rope = kv_norm_g is not None, cs_tab is not None
    kw, vw = MLA_HEADS * QK_PAD, MLA_HEADS * V_HEAD
    const = lambda i: (0, 0)
    in_specs = [pl.BlockSpec((tm, KV_LORA), lambda i: (i, ckv_blk)),
                pl.BlockSpec((tm, kr_w), lambda i: (i, kr_blk))]
    args = [ckv_src, kr_src]
    if norm:
        in_specs.append(pl.BlockSpec((1, KV_LORA), const))
        args.append(kv_norm_g.reshape(1, KV_LORA))
    if rope:
        in_specs.append(pl.BlockSpec((tm, LANES), lambda i: (i % per_seq, 0)))
        args.append(cs_tab)
    in_specs += [pl.BlockSpec((KV_LORA, kw), const), pl.BlockSpec((KV_LORA, vw), const),
                 pl.BlockSpec((QK_ROPE, kw), const)]
    args += [wk_full, wv, e_mat]
    out_specs = [pl.BlockSpec((tm, kw), lambda i: (i, 0)), pl.BlockSpec((tm, vw), lambda i: (i, 0))]
    out_shape = [jax.ShapeDtypeStruct((t, kw), BF16), jax.ShapeDtypeStruct((t, vw), BF16)]
    if emit_new:
        out_specs += [pl.BlockSpec((tm, KV_LORA), lambda i: (i, 0)), pl.BlockSpec((tm, QK_ROPE), lambda i: (i, 0))]
        out_shape += [jax.ShapeDtypeStruct((t, KV_LORA), F32), jax.ShapeDtypeStruct((t, QK_ROPE), F32)]
    return pl.pallas_call(
        functools.partial(_kv_kernel, norm=norm, rope=rope, emit_new=emit_new),
        grid=(t // tm,),
        in_specs=in_specs, out_specs=out_specs, out_shape=out_shape,
        compiler_params=_cp(("arbitrary",)),
        name="kv_proj",
    )(*args)


def _attn_kernel(*refs, cache, scale):
    if cache:
        q_ref, k_ref, v_ref, kc_ref, vc_ref, o_ref = refs
    else:
        q_ref, k_ref, v_ref, o_ref = refs
    q = q_ref[...]
    s = _dot_nt(q, k_ref[...]) * scale
    m = jnp.max(s, -1, keepdims=True)
    if cache:
        sc = _dot_nt(q, kc_ref[...]) * scale
        m = jnp.maximum(m, jnp.max(sc, -1, keepdims=True))
    p = jnp.exp(s - m)
    l = jnp.sum(p, -1, keepdims=True)
    if cache:
        pc = jnp.exp(sc - m)
        l = l + jnp.sum(pc, -1, keepdims=True)
    inv = 1.0 / l
    acc = _dot((p * inv).astype(BF16), v_ref[...])
    if cache:
        acc = acc + _dot((pc * inv).astype(BF16), vc_ref[...])
    o_ref[...] = acc.astype(BF16)


def _attention(q, k, v, kc, vc, batch, n_tok):
    tq = 256
    n_q = n_tok // tq
    cache = kc is not None
    in_specs = [pl.BlockSpec((tq, QK_PAD), lambda b, h, i: (b * n_q + i, h)),
                pl.BlockSpec((n_tok, QK_PAD), lambda b, h, i: (b, h)),
                pl.BlockSpec((n_tok, V_HEAD), lambda b, h, i: (b, h))]
    args = [q, k, v]
    if cache:
        n_c = kc.shape[0] // batch
        in_specs += [pl.BlockSpec((n_c, QK_PAD), lambda b, h, i: (b, h)),
                     pl.BlockSpec((n_c, V_HEAD), lambda b, h, i: (b, h))]
        args += [kc, vc]
    return pl.pallas_call(
        functools.partial(_attn_kernel, cache=cache, scale=1.0 / math.sqrt(QK_NOPE + QK_ROPE)),
        grid=(batch, MLA_HEADS, n_q),
        in_specs=in_specs,
        out_specs=pl.BlockSpec((tq, V_HEAD), lambda b, h, i: (b * n_q + i, h)),
        out_shape=jax.ShapeDtypeStruct((batch * n_tok, MLA_HEADS * V_HEAD), BF16),
        compiler_params=_cp(("arbitrary", "arbitrary", "arbitrary")),
        name="attention",
    )(*args)


def _pair_blockdiag(y, lane_lo):
    zero = jnp.zeros_like(y)
    return jnp.concatenate([jnp.where(lane_lo, y, zero), jnp.where(lane_lo, zero, y)], axis=0)


def _scan_kernel(*refs, tm, npair, reverse, zero_init, emit_state, finalize):
    it = iter(refs)
    blocks = {}
    cols = ("r", "k", "v", "wa") + (("gd",) if finalize else ())
    for name in cols:
        blocks[name] = (next(it), next(it), next(it), next(it))
    kk_ref, ka_ref, w0_ref, wup_ref, a0_ref, aup_ref = (next(it) for _ in range(6))
    if finalize:
        rk_ref, a0o_ref, aupo_ref, gup_ref, gng_ref, gnb_ref, yo_ref = (next(it) for _ in range(7))
    ones_ref, tri_ref = next(it), next(it)
    s0_ref = None if zero_init else next(it)
    y_ref = next(it)
    sout_ref = next(it) if emit_state else None
    s_scr = next(it)

    t_idx = pl.program_id(2)
    n_t = pl.num_programs(2)
    tile = (n_t - 1 - t_idx) if reverse else t_idx
    first_tok = tile == 0
    last_tok = tile == n_t - 1

    @pl.when(t_idx == 0)
    def _():
        if zero_init:
            s_scr[...] = jnp.zeros_like(s_scr)
        else:
            s_scr[...] = s0_ref[...]

    def shifted(name):
        main, prv, nxt, mu = blocks[name]
        u = main[...]
        row = lax.broadcasted_iota(jnp.int32, u.shape, 0)
        p_row = jnp.where(first_tok, 0.0, prv[7:8, :])
        n_row = jnp.where(last_tok, 0.0, nxt[0:1, :])
        up = jnp.where(row == 0, p_row, pltpu.roll(u, 1, 0))
        un = jnp.where(row == tm - 1, n_row, pltpu.roll(u, tm - 1, 0))
        return u + mu[...] * (0.5 * (up + un) - u)

    ones_bd = ones_ref[...]

    def seg(x):
        return jnp.concatenate(
            [_dot3_right(x[:, j * LANES:(j + 1) * LANES], ones_bd) for j in range(npair)], axis=1)

    r, k, v, wa = shifted("r"), shifted("k"), shifted("v"), shifted("wa")
    wab = wa.astype(BF16)
    kk = k * kk_ref[...]
    kk = kk * lax.rsqrt(jnp.maximum(seg(kk * kk), 1e-24))
    ld = _sigmoid(w0_ref[...] + _dot(jnp.tanh(wa).astype(BF16), wup_ref[...])) * (-math.exp(-0.5))
    a = _sigmoid(a0_ref[...] + _dot(wab, aup_ref[...]))
    kd = k * (1.0 + (a - 1.0) * ka_ref[...])
    b = kk * a

    c_row = lax.broadcasted_iota(jnp.int32, (CHUNK, LANES), 0)
    c_lane = lax.broadcasted_iota(jnp.int32, (CHUNK, LANES), 1)
    c_src = c_lane & (CHUNK - 1)
    lane_lo = c_lane < CHUNK
    strict = (c_src > c_row) if reverse else (c_src < c_row)
    incl = (c_src >= c_row) if reverse else (c_src <= c_row)
    eye = (c_src == c_row).astype(F32)
    bd_row = lax.broadcasted_iota(jnp.int32, (LANES, LANES), 0)
    bd_lane = lax.broadcasted_iota(jnp.int32, (LANES, LANES), 1)
    bd_mask = jnp.right_shift(bd_row, 6) == jnp.right_shift(bd_lane, 6)
    tri = tri_ref[...]

    def nn(p, y):
        return _dot(p.astype(BF16), _pair_blockdiag(y, lane_lo).astype(BF16))

    n_chunks = tm // CHUNK
    order = range(n_chunks - 1, -1, -1) if reverse else range(n_chunks)
    cums = [_dot3_left(tri, ld[c * CHUNK:(c + 1) * CHUNK]) for c in range(n_chunks)]
    cells = [(j, c) for j in range(npair) for c in range(n_chunks)]
    st = {}
    for j, c in cells:
        ln = slice(j * LANES, (j + 1) * LANES)
        sl = slice(c * CHUNK, (c + 1) * CHUNK)
        r_c, v_c, kk_c, kd_c, b_c, ld_c = r[sl, ln], v[sl, ln], kk[sl, ln], kd[sl, ln], b[sl, ln], ld[sl, ln]
        cum = cums[c][:, ln]
        tot = cum[0:1] if reverse else cum[CHUNK - 1:CHUNK]
        e_neg = jnp.exp(-cum)
        e_tot = jnp.exp(tot - cum)
        a_hat = -kk_c * jnp.exp(cum - ld_c)
        r_hat = r_c * jnp.exp(cum)
        ar = jnp.concatenate([a_hat, r_hat], axis=0).astype(BF16)
        gb = _dot_nt(ar, _pair_blockdiag(b_c * e_neg, lane_lo).astype(BF16))
        gk = _dot_nt(ar, _pair_blockdiag(kd_c * e_neg, lane_lo).astype(BF16))
        lmat = jnp.where(strict, gb[:CHUNK], 0.0)
        st[j, c] = dict(
            ar=ar, v=v_c, p=eye + lmat, q=lmat,
            akm=jnp.where(strict, gk[:CHUNK], 0.0),
            rbk=jnp.concatenate([jnp.where(incl, gb[CHUNK:], 0.0), jnp.where(incl, gk[CHUNK:], 0.0)],
                                axis=1).astype(BF16),
            bk=jnp.concatenate([b_c * e_tot, kd_c * e_tot], axis=0).astype(BF16),
            w=jnp.exp(tot))
    for _ in range(int(math.log2(CHUNK)) - 1):
        for cell in cells:
            st[cell]["q"] = nn(st[cell]["q"], st[cell]["q"])
        for cell in cells:
            st[cell]["p"] = st[cell]["p"] + nn(st[cell]["p"], st[cell]["q"])
    for cell in cells:
        st[cell]["akv"] = nn(st[cell]["akm"], st[cell]["v"])
    s_cur = [s_scr[j] for j in range(npair)]
    y_cells = {}
    for c in order:
        xs = [_dot_nt(st[j, c]["ar"], s_cur[j].astype(BF16)) for j in range(npair)]
        u = [nn(st[j, c]["p"], xs[j][:CHUNK] + st[j, c]["akv"]) for j in range(npair)]
        for j in range(npair):
            d = st[j, c]
            y_cells[j, c] = xs[j][CHUNK:] + _dot(
                d["rbk"],
                jnp.concatenate([_pair_blockdiag(u[j], lane_lo), _pair_blockdiag(d["v"], lane_lo)],
                                axis=0).astype(BF16))
            upd = _dot_tn(jnp.concatenate([u[j], d["v"]], axis=0).astype(BF16), d["bk"])
            s_cur[j] = s_cur[j] * d["w"] + jnp.where(bd_mask, upd, 0.0)
    s_new = s_cur
    for j in range(npair):
        s_scr[j] = s_new[j]
    y = jnp.concatenate(
        [jnp.concatenate([y_cells[j, c] for c in range(n_chunks)], axis=0) for j in range(npair)], axis=1)

    if emit_state:
        @pl.when(t_idx == n_t - 1)
        def _():
            for j in range(npair):
                sout_ref[j] = s_new[j]

    if not finalize:
        y_ref[...] = y
    else:
        rk = rk_ref[...]
        a_o = _sigmoid(a0o_ref[...] + _dot(wab, aupo_ref[...]))
        kd_o = k * (1.0 + (a_o - 1.0) * ka_ref[...])
        bonus = seg(r * kd * rk) + seg(r * kd_o * rk)
        gate = _dot(_sigmoid(shifted("gd")).astype(BF16), gup_ref[...])
        yt = y + yo_ref[...]
        inv_n = 1.0 / RWKV_HEAD
        d = yt - seg(yt) * inv_n
        var = seg(d * d) * inv_n
        yn = d * lax.rsqrt(var + GN_EPS) * gng_ref[...] + gnb_ref[...]
        y_ref[...] = ((yn + bonus * v) * gate).astype(BF16)


def _scan(proj_rw, batch, n_tok, prm, direction, s0_bd, emit_state, y_other):
    reverse = direction == 1
    finalize = y_other is not None
    zero_init = s0_bd is None
    t = batch * n_tok
    tm = 256
    npair = SCAN_PAIRS
    width = npair * LANES
    groups = HEAD_PAIRS // npair
    n_t = n_tok // tm
    row8 = tm // 8
    n_row8 = t // 8

    def tile_of(ti):
        return (n_t - 1 - ti) if reverse else ti

    def token_specs(w, cb):
        return [pl.BlockSpec((tm, w), lambda bb, p, ti: (bb * n_t + tile_of(ti), cb(p))),
                pl.BlockSpec((8, w), lambda bb, p, ti: (jnp.maximum((bb * n_t + tile_of(ti)) * row8 - 1, 0), cb(p))),
                pl.BlockSpec((8, w), lambda bb, p, ti: (jnp.minimum((bb * n_t + tile_of(ti) + 1) * row8, n_row8 - 1),
                                                        cb(p))),
                pl.BlockSpec((1, w), lambda bb, p, ti: (0, cb(p)))]

    rowvec = lambda: pl.BlockSpec((1, width), lambda bb, p, ti: (0, p))
    lora = lambda: pl.BlockSpec((LANES, width), lambda bb, p, ti: (0, p))
    const = lambda shape: pl.BlockSpec(shape, lambda bb, p, ti: (0, 0))
    state = lambda: pl.BlockSpec((None, npair, LANES, LANES), lambda bb, p, ti: (bb, p, 0, 0))
    y_spec = lambda: pl.BlockSpec((tm, width), lambda bb, p, ti: (bb * n_t + tile_of(ti), p))

    lora_blk = 3 * RWKV_WIDTH // LANES
    in_specs, args = [], []
    for name, w, cb in (("r", width, lambda p: p), ("k", width, lambda p: groups + p),
                        ("v", width, lambda p: 2 * groups + p), ("wa", LANES, lambda p: lora_blk),
                        ("gd", LANES, lambda p: lora_blk + 1)):
        if name == "gd" and not finalize:
            continue
        in_specs += token_specs(w, cb)
        args += [proj_rw, proj_rw, proj_rw, prm["mu"]]
    d, o = prm["dirs"][direction], prm["dirs"][1 - direction]
    in_specs += [rowvec(), rowvec(), rowvec(), lora(), rowvec(), lora()]
    args += [prm["k_k"], prm["k_a"], d["w0"], d["w_up"], d["a0"], d["a_up"]]
    if finalize:
        in_specs += [rowvec(), rowvec(), lora(), lora(), rowvec(), rowvec(), y_spec()]
        args += [prm["r_k"], o["a0"], o["a_up"], prm["g_up"], prm["gn_g"], prm["gn_b"], y_other]
    in_specs += [const((LANES, LANES)), const((CHUNK, CHUNK))]
    args += [prm["ones_bd"], prm["tri"][direction]]
    if not zero_init:
        in_specs.append(state())
        args.append(s0_bd)
    out_specs = [y_spec()]
    out_shape = [jax.ShapeDtypeStruct((t, RWKV_WIDTH), BF16 if finalize else F32)]
    if emit_state:
        out_specs.append(state())
        out_shape.append(jax.ShapeDtypeStruct((batch, HEAD_PAIRS, LANES, LANES), F32))
    out = pl.pallas_call(
        functools.partial(_scan_kernel, tm=tm, npair=npair, reverse=reverse, zero_init=zero_init,
                          emit_state=emit_state, finalize=finalize),
        grid=(batch, groups, n_t),
        in_specs=in_specs, out_specs=out_specs, out_shape=out_shape,
        scratch_shapes=[pltpu.VMEM((npair, LANES, LANES), F32)],
        compiler_params=_cp(("arbitrary", "arbitrary", "arbitrary")),
        name="rwkv_scan_bwd" if reverse else "rwkv_scan_fwd",
    )(*args)
    return out if emit_state else (out[0], None)


def _out_kernel(att_ref, rw_ref, x_ref, g_ref, wa_ref, wr_ref, lg_ref, lb_ref, o_ref):
    f = _dot(att_ref[...], wa_ref[...]) + _dot(rw_ref[...], wr_ref[...])
    o_ref[...] = _layer_norm(ALPHA * x_ref[...] + g_ref[...] * f, lg_ref[...], lb_ref[...])


def _out_proj(att, rw, x2, mod4, seq0, n_seq_tokens, w_att, w_rw, ln_g, ln_b):
    t, tm = x2.shape[0], 256
    per_seq = n_seq_tokens // tm
    seq_of = lambda i: seq0 + i // per_seq
    half = D_MODEL // 2
    const = lambda i: (0, 0)
    return pl.pallas_call(
        _out_kernel,
        grid=(t // tm,),
        in_specs=[pl.BlockSpec((tm, half), lambda i: (i, 0)), pl.BlockSpec((tm, half), lambda i: (i, 0)),
                  pl.BlockSpec((tm, D_MODEL), lambda i: (i, 0)), _mod_spec(2, seq_of),
                  pl.BlockSpec((half, D_MODEL), const), pl.BlockSpec((half, D_MODEL), const),
                  pl.BlockSpec((1, D_MODEL), const), pl.BlockSpec((1, D_MODEL), const)],
        out_specs=pl.BlockSpec((tm, D_MODEL), lambda i: (i, 0)),
        out_shape=jax.ShapeDtypeStruct((t, D_MODEL), F32),
        compiler_params=_cp(("arbitrary",)),
        name="out_proj",
    )(att, rw, x2, mod4, w_att, w_rw, ln_g.reshape(1, D_MODEL), ln_b.reshape(1, D_MODEL))


def _ffn_kernel(x_ref, sh_ref, sc_ref, g_ref, wg_ref, wu_ref, wd_ref, lg_ref, lb_ref, o_ref, h_scr, acc_scr):
    kf = pl.program_id(1)

    @pl.when(kf == 0)
    def _():
        h_scr[...] = (x_ref[...] * (1.0 + sc_ref[...]) + sh_ref[...]).astype(BF16)
        acc_scr[...] = jnp.zeros_like(acc_scr)

    h = h_scr[...]
    gate = _dot(h, wg_ref[...])
    act = (gate * _sigmoid(gate) * _dot(h, wu_ref[...])).astype(BF16)
    acc_scr[...] += _dot(act, wd_ref[...])

    @pl.when(kf == pl.num_programs(1) - 1)
    def _():
        o_ref[...] = _layer_norm(ALPHA * x_ref[...] + g_ref[...] * acc_scr[...], lg_ref[...], lb_ref[...])


def _ffn(x1, mod4, seq0, n_seq_tokens, w_gate, w_up, w_down, ln_g, ln_b):
    t, tm, tf = x1.shape[0], 512, 512
    tm = min(tm, n_seq_tokens)
    per_seq = n_seq_tokens // tm
    seq_of = lambda i, kf: seq0 + i // per_seq
    const = lambda i, kf: (0, 0)
    return pl.pallas_call(
        _ffn_kernel,
        grid=(t // tm, D_FF // tf),
        in_specs=[pl.BlockSpec((tm, D_MODEL), lambda i, kf: (i, 0)),
                  _mod_spec(3, seq_of), _mod_spec(4, seq_of), _mod_spec(5, seq_of),
                  pl.BlockSpec((D_MODEL, tf), lambda i, kf: (0, kf)),
                  pl.BlockSpec((D_MODEL, tf), lambda i, kf: (0, kf)),
                  pl.BlockSpec((tf, D_MODEL), lambda i, kf: (kf, 0)),
                  pl.BlockSpec((1, D_MODEL), const), pl.BlockSpec((1, D_MODEL), const)],
        out_specs=pl.BlockSpec((tm, D_MODEL), lambda i, kf: (i, 0)),
        out_shape=jax.ShapeDtypeStruct((t, D_MODEL), F32),
        scratch_shapes=[pltpu.VMEM((tm, D_MODEL), BF16), pltpu.VMEM((tm, D_MODEL), F32)],
        compiler_params=_cp(("arbitrary", "arbitrary")),
        name="ffn",
    )(x1, mod4, mod4, mod4, w_gate, w_up, w_down, ln_g.reshape(1, D_MODEL), ln_b.reshape(1, D_MODEL))


def _rope_tables(n_tokens):
    rows = n_tokens // GRID_W
    row = jnp.repeat(jnp.arange(rows), GRID_W).astype(F32)
    col = jnp.tile(jnp.arange(GRID_W), rows).astype(F32)
    freqs = ROPE_THETA ** (-jnp.arange(ROPE_AXIS_FREQS, dtype=F32) / ROPE_AXIS_FREQS)
    ang = jnp.concatenate([row[:, None] * freqs, col[:, None] * freqs], -1)
    cos, sin = jnp.cos(ang), jnp.sin(ang)
    cos2 = jnp.concatenate([cos, cos], -1)
    sin2 = jnp.concatenate([-sin, sin], -1)
    pad_lo = jnp.zeros((n_tokens, QK_NOPE), F32)
    pad_hi = jnp.zeros((n_tokens, QK_PAD - QK_NOPE - QK_ROPE), F32)
    q_cos = jnp.concatenate([pad_lo + 1.0, cos2, pad_hi], -1)
    q_sin = jnp.concatenate([pad_lo, sin2, pad_hi], -1)
    k_tab = jnp.concatenate([cos2, sin2], -1)
    return q_cos, q_sin, k_tab


def _swap_halves(w):
    half = w.shape[-1] // 2
    return jnp.concatenate([w[..., half:], w[..., :half]], -1)


def _block_diag_state(s):
    bsz = s.shape[0]
    s = s.reshape(bsz, HEAD_PAIRS, 2, RWKV_HEAD, RWKV_HEAD)
    z = jnp.zeros_like(s[:, :, 0])
    top = jnp.concatenate([s[:, :, 0], z], -1)
    bot = jnp.concatenate([z, s[:, :, 1]], -1)
    return jnp.concatenate([top, bot], -2)


def _unblock_diag_state(s_bd):
    bsz = s_bd.shape[0]
    s = jnp.stack([s_bd[:, :, :RWKV_HEAD, :RWKV_HEAD], s_bd[:, :, RWKV_HEAD:, RWKV_HEAD:]], 2)
    return s.reshape(bsz, RWKV_HEADS, RWKV_HEAD, RWKV_HEAD)


def _layer(x2, batch, n_tok, mod4, seq0, per_seq_mod, wts, rope_tabs, cache, states, is_context):
    mod_tokens = n_tok if per_seq_mod else 0
    proj_mla, proj_rw = _in_proj(x2, mod4, seq0, mod_tokens, wts["w_mla"], wts["w_rw"])
    q = _q_proj(proj_mla, n_tok, wts["q_norm_g"], wts["wq_full"], wts["wq_sw"],
                None if rope_tabs is None else rope_tabs[:2])
    kv = _kv_proj(proj_mla, OFF_KV // KV_LORA, proj_mla, OFF_KR // LANES, LANES, n_tok, wts["kv_norm_g"],
                  None if rope_tabs is None else rope_tabs[2], wts["wk_full"], wts["w_uv"], wts["e_mat"], is_context)
    kc = vc = None
    if cache is not None:
        kc, vc = _kv_proj(cache[0], 0, cache[1], 0, QK_ROPE, cache[0].shape[0] // batch, None, None,
                          wts["wk_full"], wts["w_uv"], wts["e_mat"], False)
    att = _attention(q, kv[0], kv[1], kc, vc, batch, n_tok)
    s_f = None if states is None else _block_diag_state(states[0])
    s_b = None if states is None else _block_diag_state(states[1])
    y_f, new_f = _scan(proj_rw, batch, n_tok, wts["rwkv"], 0, s_f, is_context, None)
    rw, new_b = _scan(proj_rw, batch, n_tok, wts["rwkv"], 1, s_b, is_context, y_f)
    seq_tokens = n_tok if per_seq_mod else x2.shape[0]
    x1 = _out_proj(att, rw, x2, mod4, seq0, seq_tokens, wts["w_out_att"], wts["w_out_rw"], wts["ln1_g"], wts["ln1_b"])
    y = _ffn(x1, mod4, seq0, seq_tokens, wts["w_ffn_gate"], wts["w_ffn_up"], wts["w_ffn_down"],
             wts["ln2_g"], wts["ln2_b"])
    return y, kv, new_f, new_b


def kernel(x_prompt, x_sample, cache_ckv, cache_krope, state_wkv_fwd, state_wkv_bwd, c, c_ctx, w_mod, b_mod, w_in, q_norm_g, kv_norm_g, w_uq, w_uk, w_uv, tok_shift_mu, w0_fwd, w_up_fwd, a0_fwd, a_up_fwd, w0_bwd, w_up_bwd, a0_bwd, a_up_bwd, g_up, k_k, k_a, r_k, gn_g, gn_b, w_out, ln1_g, ln1_b, w_ffn_gate, w_ffn_up, w_ffn_down, ln2_g, ln2_b):
    batch, seq = x_prompt.shape[:2]
    dec_batch, dec_seq = x_sample.shape[:2]
    layer = 0

    mod_rows = 16
    cond = jnp.concatenate([c, c_ctx[None, :], jnp.zeros((mod_rows - dec_batch - 1, D_MODEL), F32)], 0)
    mod4 = _modulation(cond, w_mod[layer], b_mod[layer]).reshape(mod_rows, 6, 1, D_MODEL)

    wi = w_in[layer]
    kr_w = wi[:, OFF_KR:OFF_RW]
    uq = w_uq[layer].reshape(Q_LORA, MLA_HEADS, QK_NOPE + QK_ROPE)
    uq_pad = jnp.zeros((Q_LORA, MLA_HEADS, QK_PAD - QK_NOPE - QK_ROPE), F32)
    wq_full = jnp.concatenate([uq, uq_pad], -1).reshape(Q_LORA, MLA_HEADS * QK_PAD)
    wq_sw = jnp.concatenate([jnp.zeros((Q_LORA, MLA_HEADS, QK_NOPE), F32), _swap_halves(uq[..., QK_NOPE:]), uq_pad],
                            -1).reshape(Q_LORA, MLA_HEADS * QK_PAD)
    uk = w_uk[layer].reshape(KV_LORA, MLA_HEADS, QK_NOPE)
    wk_full = jnp.concatenate([uk, jnp.zeros((KV_LORA, MLA_HEADS, QK_PAD - QK_NOPE), F32)], -1)
    e_head = jnp.concatenate([jnp.zeros((QK_ROPE, QK_NOPE), F32), jnp.eye(QK_ROPE, dtype=F32),
                              jnp.zeros((QK_ROPE, QK_PAD - QK_NOPE - QK_ROPE), F32)], -1)
    lora_pad = jnp.zeros((DECAY_LORA, RWKV_WIDTH), F32)
    row = lambda vec: vec[layer].reshape(1, -1)
    head_of = jnp.arange(LANES) // RWKV_HEAD
    tri_lo = jnp.tril(jnp.ones((CHUNK, CHUNK), F32))
    rwkv = {
        "mu": row(tok_shift_mu), "k_k": row(k_k), "k_a": row(k_a), "r_k": row(r_k),
        "gn_g": row(gn_g), "gn_b": row(gn_b), "g_up": g_up[layer].astype(BF16),
        "dirs": [
            {"w0": row(w0_fwd), "a0": row(a0_fwd),
             "w_up": jnp.concatenate([w_up_fwd[layer], lora_pad], 0).astype(BF16),
             "a_up": jnp.concatenate([lora_pad, a_up_fwd[layer]], 0).astype(BF16)},
            {"w0": row(w0_bwd), "a0": row(a0_bwd),
             "w_up": jnp.concatenate([w_up_bwd[layer], lora_pad], 0).astype(BF16),
             "a_up": jnp.concatenate([lora_pad, a_up_bwd[layer]], 0).astype(BF16)}],
        "ones_bd": (head_of[:, None] == head_of[None, :]).astype(BF16),
        "tri": [tri_lo.astype(BF16), tri_lo.T.astype(BF16)],
    }
    wts = {
        "w_mla": jnp.concatenate([wi[:, :OFF_RW], _swap_halves(kr_w)], -1).astype(BF16),
        "w_rw": wi[:, OFF_RW:].astype(BF16),
        "q_norm_g": q_norm_g[layer], "kv_norm_g": kv_norm_g[layer],
        "wq_full": wq_full.astype(BF16), "wq_sw": wq_sw.astype(BF16),
        "wk_full": wk_full.reshape(KV_LORA, MLA_HEADS * QK_PAD).astype(BF16),
        "w_uv": w_uv[layer].astype(BF16),
        "e_mat": jnp.tile(e_head, (1, MLA_HEADS)).astype(BF16),
        "rwkv": rwkv,
        "w_out_att": w_out[layer][:MLA_HEADS * V_HEAD].astype(BF16),
        "w_out_rw": w_out[layer][MLA_HEADS * V_HEAD:].astype(BF16),
        "ln1_g": ln1_g[layer], "ln1_b": ln1_b[layer],
        "w_ffn_gate": w_ffn_gate[layer].astype(BF16), "w_ffn_up": w_ffn_up[layer].astype(BF16),
        "w_ffn_down": w_ffn_down[layer].astype(BF16),
        "ln2_g": ln2_g[layer], "ln2_b": ln2_b[layer],
    }

    y_prompt, kv_ctx, s_f, s_b = _layer(
        x_prompt.reshape(batch * seq, D_MODEL), batch, seq, mod4, dec_batch, False, wts, None, None, None, True)
    past = cache_ckv.shape[2]
    cache = (cache_ckv[:, layer].reshape(dec_batch * past, KV_LORA),
             cache_krope[:, layer].reshape(dec_batch * past, QK_ROPE))
    y_sample, _, _, _ = _layer(
        x_sample.reshape(dec_batch * dec_seq, D_MODEL), dec_batch, dec_seq, mod4, 0, True, wts,
        _rope_tables(dec_seq), cache, (state_wkv_fwd[:, layer], state_wkv_bwd[:, layer]), False)

    return (y_prompt.reshape(batch, seq, D_MODEL),
            y_sample.reshape(dec_batch, dec_seq, D_MODEL),
            kv_ctx[2].reshape(batch, 1, seq, KV_LORA),
            kv_ctx[3].reshape(batch, 1, seq, QK_ROPE),
            _unblock_diag_state(s_f)[:, None],
            _unblock_diag_state(s_b)[:, None])
```

```python
import functools
import math

import jax
import jax.numpy as jnp
from jax import lax
from jax.experimental import pallas as pl
from jax.experimental.pallas import tpu as pltpu

F32 = jnp.float32
BF16 = jnp.bfloat16

D_MODEL = 2048
GRID_W = 64
MLA_HEADS = 8
QK_NOPE = 128
QK_ROPE = 64
V_HEAD = 128
Q_LORA = 512
KV_LORA = 256
ROPE_AXIS_FREQS = QK_ROPE // 4
ROPE_THETA = 10000.0
RWKV_HEADS = 16
RWKV_HEAD = 64
RWKV_WIDTH = RWKV_HEADS * RWKV_HEAD
DECAY_LORA = 64
ICLR_LORA = 64
GATE_LORA = 128
OFF_KV = Q_LORA
OFF_KR = OFF_KV + KV_LORA
OFF_RW = OFF_KR + QK_ROPE
RW_COLS = 3 * RWKV_WIDTH + DECAY_LORA + ICLR_LORA + GATE_LORA
D_FF = 5632
LN_EPS = 1e-5
RMS_EPS = 1e-6
GN_EPS = 64e-5
DEPTH = 1
ALPHA = (2.0 * DEPTH) ** 0.25

LANES = 128
QK_PAD = 2 * LANES
MLA_COLS = OFF_RW + QK_ROPE
CHUNK = 64
HEAD_PAIRS = RWKV_HEADS // 2
ATTN_TQ = 512
ATTN_HEADS = 2
SCAN_PAIRS = 8
VMEM_LIMIT = 56 * 1024 * 1024


def _cp(sem):
    return pltpu.CompilerParams(dimension_semantics=sem, vmem_limit_bytes=VMEM_LIMIT)


def _dot(a, b):
    return jnp.dot(a, b, preferred_element_type=F32)


def _dot_nt(a, b):
    return lax.dot_general(a, b, (((1,), (1,)), ((), ())), preferred_element_type=F32)


def _dot_tn(a, b):
    return lax.dot_general(a, b, (((0,), (0,)), ((), ())), preferred_element_type=F32)


def _sigmoid(x):
    return 1.0 / (1.0 + jnp.exp(-x))


def _layer_norm(h, g, b):
    mu = jnp.mean(h, -1, keepdims=True)
    d = h - mu
    var = jnp.mean(d * d, -1, keepdims=True)
    return d * lax.rsqrt(var + LN_EPS) * g + b


def _split3(x):
    hi = x.astype(BF16)
    r1 = x - hi.astype(F32)
    mid = r1.astype(BF16)
    lo = (r1 - mid.astype(F32)).astype(BF16)
    return hi, mid, lo


def _dot2_right(x, m):
    hi = x.astype(BF16)
    lo = (x - hi.astype(F32)).astype(BF16)
    return _dot(hi, m) + _dot(lo, m)


def _dot3_left(m, x):
    hi, mid, lo = _split3(x)
    return _dot(m, hi) + _dot(m, mid) + _dot(m, lo)


def _mod_kernel(c_ref, w_ref, b_ref, o_ref):
    c = c_ref[...]
    s = (c * _sigmoid(c)).astype(BF16)
    o_ref[...] = _dot(s, w_ref[...].astype(BF16)) + b_ref[...]


def _modulation(cond, w_mod, b_mod):
    rows, tn = cond.shape[0], 1024
    n = w_mod.shape[1]
    return pl.pallas_call(
        _mod_kernel,
        grid=(n // tn,),
        in_specs=[pl.BlockSpec((rows, D_MODEL), lambda j: (0, 0)),
                  pl.BlockSpec((D_MODEL, tn), lambda j: (0, j)),
                  pl.BlockSpec((1, tn), lambda j: (0, j))],
        out_specs=pl.BlockSpec((rows, tn), lambda j: (0, j)),
        out_shape=jax.ShapeDtypeStruct((rows, n), F32),
        compiler_params=_cp(("arbitrary",)),
        name="modulation",
    )(cond, w_mod, b_mod.reshape(1, n))


def _mod_spec(which, seq_of):
    return pl.BlockSpec((None, None, 1, D_MODEL), lambda *g: (seq_of(*g), which, 0, 0))


def _in_kernel(x_ref, sh_ref, sc_ref, wm_ref, wr_ref, om_ref, or_ref):
    xm = (x_ref[...] * (1.0 + sc_ref[...]) + sh_ref[...]).astype(BF16)
    om_ref[...] = _dot(xm, wm_ref[...])
    or_ref[...] = _dot(xm, wr_ref[...])


def _in_proj(x2, mod4, seq0, n_seq_tokens, w_mla, w_rw):
    t, tm = x2.shape[0], 256
    per_seq = n_seq_tokens // tm
    seq_of = (lambda i: seq0) if per_seq == 0 else (lambda i: seq0 + i // per_seq)
    return pl.pallas_call(
        _in_kernel,
        grid=(t // tm,),
        in_specs=[pl.BlockSpec((tm, D_MODEL), lambda i: (i, 0)),
                  _mod_spec(0, seq_of), _mod_spec(1, seq_of),
                  pl.BlockSpec((D_MODEL, MLA_COLS), lambda i: (0, 0)),
                  pl.BlockSpec((D_MODEL, RW_COLS), lambda i: (0, 0))],
        out_specs=[pl.BlockSpec((tm, MLA_COLS), lambda i: (i, 0)),
                   pl.BlockSpec((tm, RW_COLS), lambda i: (i, 0))],
        out_shape=[jax.ShapeDtypeStruct((t, MLA_COLS), F32),
                   jax.ShapeDtypeStruct((t, RW_COLS), F32)],
        compiler_params=_cp(("arbitrary",)),
        name="in_proj",
    )(x2, mod4, mod4, w_mla, w_rw)


def _q_kernel(*refs, rope):
    if rope:
        qd_ref, g_ref, w_ref, wsw_ref, c_ref, s_ref, o_ref = refs
    else:
        qd_ref, g_ref, w_ref, o_ref = refs
    x = qd_ref[...]
    qn = (x * lax.rsqrt(jnp.mean(x * x, -1, keepdims=True) + RMS_EPS) * g_ref[...]).astype(BF16)
    q = _dot(qn, w_ref[...])
    if rope:
        c = jnp.concatenate([c_ref[...]] * MLA_HEADS, axis=1)
        s = jnp.concatenate([s_ref[...]] * MLA_HEADS, axis=1)
        q = q * c + _dot(qn, wsw_ref[...]) * s
    o_ref[...] = q.astype(BF16)


def _q_proj(proj_mla, n_seq_tokens, q_norm_g, wq_full, wq_sw, rope_tabs):
    t, tm = proj_mla.shape[0], 256
    per_seq = n_seq_tokens // tm
    rope = rope_tabs is not None
    width = MLA_HEADS * QK_PAD
    const = lambda i: (0, 0)
    in_specs = [pl.BlockSpec((tm, Q_LORA), lambda i: (i, 0)),
                pl.BlockSpec((1, Q_LORA), const),
                pl.BlockSpec((Q_LORA, width), const)]
    args = [proj_mla, q_norm_g.reshape(1, Q_LORA), wq_full]
    if rope:
        in_specs += [pl.BlockSpec((Q_LORA, width), const),
                     pl.BlockSpec((tm, QK_PAD), lambda i: (i % per_seq, 0)),
                     pl.BlockSpec((tm, QK_PAD), lambda i: (i % per_seq, 0))]
        args += [wq_sw, rope_tabs[0], rope_tabs[1]]
    return pl.pallas_call(
        functools.partial(_q_kernel, rope=rope),
        grid=(t // tm,),
        in_specs=in_specs,
        out_specs=pl.BlockSpec((tm, width), lambda i: (i, 0)),
        out_shape=jax.ShapeDtypeStruct((t, width), BF16),
        compiler_params=_cp(("arbitrary",)),
        name="q_proj",
    )(*args)


def _kv_kernel(*refs, norm, rope, emit_new):
    it = iter(refs)
    ckv_ref, kr_ref = next(it), next(it)
    g_ref = next(it) if norm else None
    cs_ref = next(it) if rope else None
    wk_ref, wv_ref, e_ref = next(it), next(it), next(it)
    k_out, v_out = next(it), next(it)
    x = ckv_ref[...]
    if norm:
        x = x * lax.rsqrt(jnp.mean(x * x, -1, keepdims=True) + RMS_EPS) * g_ref[...]
    xb = x.astype(BF16)
    kr = kr_ref[...]
    if rope:
        y = kr * cs_ref[...]
        kr = y + pltpu.roll(y, QK_ROPE, 1)
    kr = kr[:, :QK_ROPE]
    k_out[...] = (_dot(xb, wk_ref[...]) + _dot(kr.astype(BF16), e_ref[...])).T.astype(BF16)
    v_out[...] = _dot(xb, wv_ref[...]).astype(BF16)
    if emit_new:
        nc_out, nk_out = next(it), next(it)
        nc_out[...] = x
        nk_out[...] = kr


def _kv_proj(ckv_src, ckv_blk, kr_src, kr_blk, kr_w, n_seq_tokens, kv_norm_g, cs_tab, wk_full, wv, e_mat, emit_new):
    t, tm = ckv_src.shape[0], 256
    per_seq = n_seq_tokens // tm
    norm, rope = kv_norm_g is not None, cs_tab is not None
    kw, vw = MLA_HEADS * QK_PAD, MLA_HEADS * V_HEAD
    const = lambda i: (0, 0)
    in_specs = [pl.BlockSpec((tm, KV_LORA), lambda i: (i, ckv_blk)),
                pl.BlockSpec((tm, kr_w), lambda i: (i, kr_blk))]
    args = [ckv_src, kr_src]
    if norm:
        in_specs.append(pl.BlockSpec((1, KV_LORA), const))
        args.append(kv_norm_g.reshape(1, KV_LORA))
    if rope:
        in_specs.append(pl.BlockSpec((tm, LANES), lambda i: (i % per_seq, 0)))
        args.append(cs_tab)
    in_specs += [pl.BlockSpec((KV_LORA, kw), const), pl.BlockSpec((KV_LORA, vw), const),
                 pl.BlockSpec((QK_ROPE, kw), const)]
    args += [wk_full, wv, e_mat]
    out_specs = [pl.BlockSpec((kw, tm), lambda i: (i // per_seq, i % per_seq)),
                 pl.BlockSpec((tm, vw), lambda i: (i, 0))]
    out_shape = [jax.ShapeDtypeStruct((t // n_seq_tokens * kw, n_seq_tokens), BF16),
                 jax.ShapeDtypeStruct((t, vw), BF16)]
    if emit_new:
        out_specs += [pl.BlockSpec((tm, KV_LORA), lambda i: (i, 0)), pl.BlockSpec((tm, QK_ROPE), lambda i: (i, 0))]
        out_shape += [jax.ShapeDtypeStruct((t, KV_LORA), F32), jax.ShapeDtypeStruct((t, QK_ROPE), F32)]
    return pl.pallas_call(
        functools.partial(_kv_kernel, norm=norm, rope=rope, emit_new=emit_new),
        grid=(t // tm,),
        in_specs=in_specs, out_specs=out_specs, out_shape=out_shape,
        compiler_params=_cp(("arbitrary",)),
        name="kv_proj",
    )(*args)


def _attn_kernel(*refs, cache, heads, exp2_scale):
    if cache:
        q_ref, k_ref, v_ref, kc_ref, vc_ref, o_ref = refs
    else:
        q_ref, k_ref, v_ref, o_ref = refs
    hs = range(heads)
    qk = lambda h: slice(h * QK_PAD, (h + 1) * QK_PAD)
    hv = lambda h: slice(h * V_HEAD, (h + 1) * V_HEAD)
    q = [q_ref[:, qk(h)] for h in hs]
    s = [_dot(q[h], k_ref[qk(h), :]) for h in hs]
    m = [jnp.max(s[h], -1, keepdims=True) for h in hs]
    if cache:
        sc = [_dot(q[h], kc_ref[qk(h), :]) for h in hs]
        m = [jnp.maximum(m[h], jnp.max(sc[h], -1, keepdims=True)) for h in hs]
    p = [jnp.exp2((s[h] - m[h]) * exp2_scale) for h in hs]
    l = [jnp.sum(p[h], -1, keepdims=True) for h in hs]
    acc = [_dot(p[h].astype(BF16), v_ref[:, hv(h)]) for h in hs]
    if cache:
        pc = [jnp.exp2((sc[h] - m[h]) * exp2_scale) for h in hs]
        l = [l[h] + jnp.sum(pc[h], -1, keepdims=True) for h in hs]
        acc = [acc[h] + _dot(pc[h].astype(BF16), vc_ref[:, hv(h)]) for h in hs]
    o_ref[...] = jnp.concatenate([acc[h] * (1.0 / l[h]) for h in hs], axis=1).astype(BF16)


def _attention(q, kt, v, kct, vc, batch, n_tok):
    tq, heads = min(ATTN_TQ, n_tok), ATTN_HEADS
    n_q = n_tok // tq
    groups = MLA_HEADS // heads
    cache = kct is not None
    in_specs = [pl.BlockSpec((tq, heads * QK_PAD), lambda b, h, i: (b * n_q + i, h)),
                pl.BlockSpec((heads * QK_PAD, n_tok), lambda b, h, i: (b * groups + h, 0)),
                pl.BlockSpec((n_tok, heads * V_HEAD), lambda b, h, i: (b, h))]
    args = [q, kt, v]
    if cache:
        n_c = vc.shape[0] // batch
        in_specs += [pl.BlockSpec((heads * QK_PAD, n_c), lambda b, h, i: (b * groups + h, 0)),
                     pl.BlockSpec((n_c, heads * V_HEAD), lambda b, h, i: (b, h))]
        args += [kct, vc]
    scale = 1.0 / math.sqrt(QK_NOPE + QK_ROPE)
    return pl.pallas_call(
        functools.partial(_attn_kernel, cache=cache, heads=heads, exp2_scale=scale * math.log2(math.e)),
        grid=(batch, groups, n_q),
        in_specs=in_specs,
        out_specs=pl.BlockSpec((tq, heads * V_HEAD), lambda b, h, i: (b * n_q + i, h)),
        out_shape=jax.ShapeDtypeStruct((batch * n_tok, MLA_HEADS * V_HEAD), BF16),
        compiler_params=_cp(("arbitrary", "arbitrary", "arbitrary")),
        name="attention",
    )(*args)


def _pair_blockdiag(y, lane_lo):
    zero = jnp.zeros_like(y)
    return jnp.concatenate([jnp.where(lane_lo, y, zero), jnp.where(lane_lo, zero, y)], axis=0)


def _scan_kernel(*refs, tm, npair, reverse, zero_init, emit_state, finalize):
    it = iter(refs)
    blocks = {}
    cols = ("r", "k", "v", "wa") + (("gd",) if finalize else ())
    for name in cols:
        blocks[name] = (next(it), next(it), next(it), next(it))
    kk_ref, ka_ref, w0_ref, wup_ref, a0_ref, aup_ref = (next(it) for _ in range(6))
    if finalize:
        rk_ref, a0o_ref, aupo_ref, gup_ref, gng_ref, gnb_ref, yo_ref = (next(it) for _ in range(7))
    ones_ref, tri_ref = next(it), next(it)
    s0_ref = None if zero_init else next(it)
    y_ref = next(it)
    sout_ref = next(it) if emit_state else None
    s_scr = next(it)

    t_idx = pl.program_id(2)
    n_t = pl.num_programs(2)
    tile = (n_t - 1 - t_idx) if reverse else t_idx
    first_tok = tile == 0
    last_tok = tile == n_t - 1

    @pl.when(t_idx == 0)
    def _():
        if zero_init:
            s_scr[...] = jnp.zeros_like(s_scr)
        else:
            s_scr[...] = s0_ref[...]

    def shifted(name):
        main, prv, nxt, mu = blocks[name]
        u = main[...]
        row = lax.broadcasted_iota(jnp.int32, u.shape, 0)
        p_row = jnp.where(first_tok, 0.0, prv[7:8, :])
        n_row = jnp.where(last_tok, 0.0, nxt[0:1, :])
        up = jnp.where(row == 0, p_row, pltpu.roll(u, 1, 0))
        un = jnp.where(row == tm - 1, n_row, pltpu.roll(u, tm - 1, 0))
        return u + mu[...] * (0.5 * (up + un) - u)

    ones_bd = ones_ref[...]

    seg_w = ones_bd.shape[0]

    def seg(x):
        return jnp.concatenate(
            [_dot2_right(x[:, j * seg_w:(j + 1) * seg_w], ones_bd) for j in range(x.shape[1] // seg_w)], axis=1)

    r, k, v, wa = shifted("r"), shifted("k"), shifted("v"), shifted("wa")
    wab = wa.astype(BF16)
    kk = k * kk_ref[...]
    kk = kk * lax.rsqrt(jnp.maximum(seg(kk * kk), 1e-24))
    ld = _sigmoid(w0_ref[...] + _dot(jnp.tanh(wa).astype(BF16), wup_ref[...])) * (-math.exp(-0.5))
    a = _sigmoid(a0_ref[...] + _dot(wab, aup_ref[...]))
    kd = k * (1.0 + (a - 1.0) * ka_ref[...])
    b = kk * a

    c_row = lax.broadcasted_iota(jnp.int32, (CHUNK, LANES), 0)
    c_lane = lax.broadcasted_iota(jnp.int32, (CHUNK, LANES), 1)
    c_src = c_lane & (CHUNK - 1)
    lane_lo = c_lane < CHUNK
    strict = (c_src > c_row) if reverse else (c_src < c_row)
    incl = (c_src >= c_row) if reverse else (c_src <= c_row)
    eye = (c_src == c_row).astype(F32)
    bd_row = lax.broadcasted_iota(jnp.int32, (LANES, LANES), 0)
    bd_lane = lax.broadcasted_iota(jnp.int32, (LANES, LANES), 1)
    bd_mask = jnp.right_shift(bd_row, 6) == jnp.right_shift(bd_lane, 6)
    tri = tri_ref[...]

    def nn(p, y):
        return _dot(p.astype(BF16), _pair_blockdiag(y, lane_lo).astype(BF16))

    n_chunks = tm // CHUNK
    order = list(range(n_chunks - 1, -1, -1) if reverse else range(n_chunks))
    cums = [_dot3_left(tri, ld[c * CHUNK:(c + 1) * CHUNK]) for c in range(n_chunks)]
    pairs = range(npair)
    strict2 = jnp.concatenate([strict, strict], axis=1)
    incl2 = jnp.concatenate([incl, incl], axis=1)
    st = {}

    def independent_stage(c):
        sl = slice(c * CHUNK, (c + 1) * CHUNK)
        for j in pairs:
            ln = slice(j * LANES, (j + 1) * LANES)
            r_c, v_c, kk_c, kd_c, b_c, ld_c = r[sl, ln], v[sl, ln], kk[sl, ln], kd[sl, ln], b[sl, ln], ld[sl, ln]
            cum = cums[c][:, ln]
            tot = cum[0:1] if reverse else cum[CHUNK - 1:CHUNK]
            e_neg = jnp.exp(-cum)
            e_tot = jnp.exp(tot - cum)
            a_hat = -kk_c * jnp.exp(cum - ld_c)
            r_hat = r_c * jnp.exp(cum)
            ar = jnp.concatenate([a_hat, r_hat], axis=0).astype(BF16)
            bk_hat = jnp.concatenate([_pair_blockdiag(b_c * e_neg, lane_lo), _pair_blockdiag(kd_c * e_neg, lane_lo)],
                                     axis=0).astype(BF16)
            g = _dot_nt(ar, bk_hat)
            gs = jnp.where(strict2, g[:CHUNK], 0.0)
            lmat = gs[:, :LANES]
            st[j, c] = dict(
                ar=ar, v=v_c.astype(BF16), v_bd=_pair_blockdiag(v_c, lane_lo).astype(BF16),
                p=eye + lmat, q=lmat, akm=gs[:, LANES:],
                rbk=jnp.where(incl2, g[CHUNK:], 0.0).astype(BF16),
                bk=jnp.concatenate([b_c * e_tot, kd_c * e_tot], axis=0).astype(BF16),
                w=jnp.exp(tot))
        yield
        for j in pairs:
            st[j, c]["q"] = nn(st[j, c]["q"], st[j, c]["q"])
        yield
        for _ in range(int(math.log2(CHUNK)) - 2):
            for j in pairs:
                d = st[j, c]
                pq = nn(jnp.concatenate([d["p"], d["q"]], axis=0), d["q"])
                d["p"], d["q"] = d["p"] + pq[:CHUNK], pq[CHUNK:]
            yield
        for j in pairs:
            d = st[j, c]
            d["p"] = d["p"] + nn(d["p"], d["q"])
            d["akv"] = _dot(d["akm"].astype(BF16), d["v_bd"])
        yield

    s_cur = [s_scr[j] for j in pairs]
    y_cells = {}

    def dependent_stage(c):
        xs = [_dot_nt(st[j, c]["ar"], s_cur[j].astype(BF16)) for j in pairs]
        yield
        u = [nn(st[j, c]["p"], xs[j][:CHUNK] + st[j, c]["akv"]) for j in pairs]
        yield
        for j in pairs:
            d = st[j, c]
            y_cells[j, c] = xs[j][CHUNK:] + _dot(
                d["rbk"],
                jnp.concatenate([_pair_blockdiag(u[j], lane_lo).astype(BF16), d["v_bd"]], axis=0))
            upd = _dot_tn(jnp.concatenate([u[j].astype(BF16), d["v"]], axis=0), d["bk"])
            s_cur[j] = s_cur[j] * d["w"] + jnp.where(bd_mask, upd, 0.0)
        yield

    def run_side_by_side(*stages):
        live = list(stages)
        while live:
            for g in list(live):
                if next(g, StopIteration) is StopIteration:
                    live.remove(g)

    run_side_by_side(independent_stage(order[0]))
    for prev_c, c in zip(order[:-1], order[1:]):
        run_side_by_side(independent_stage(c), dependent_stage(prev_c))
    run_side_by_side(dependent_stage(order[-1]))
    s_new = s_cur
    for j in range(npair):
        s_scr[j] = s_new[j]
    y = jnp.concatenate(
        [jnp.concatenate([y_cells[j, c] for c in range(n_chunks)], axis=0) for j in range(npair)], axis=1)

    if emit_state:
        @pl.when(t_idx == n_t - 1)
        def _():
            for j in range(npair):
                sout_ref[j] = s_new[j]

    if not finalize:
        y_ref[...] = y
    else:
        rk = rk_ref[...]
        a_o = _sigmoid(a0o_ref[...] + _dot(wab, aupo_ref[...]))
        kd_o = k * (1.0 + (a_o - 1.0) * ka_ref[...])
        bonus = seg(r * rk * (kd + kd_o))
        gate = _dot(_sigmoid(shifted("gd")).astype(BF16), gup_ref[...])
        yt = y + yo_ref[...]
        inv_n = 1.0 / RWKV_HEAD
        d = yt - seg(yt) * inv_n
        var = seg(d * d) * inv_n
        yn = d * lax.rsqrt(var + GN_EPS) * gng_ref[...] + gnb_ref[...]
        y_ref[...] = ((yn + bonus * v) * gate).astype(BF16)


def _scan(proj_rw, batch, n_tok, prm, direction, s0_bd, emit_state, y_other):
    reverse = direction == 1
    finalize = y_other is not None
    zero_init = s0_bd is None
    t = batch * n_tok
    tm = 256
    npair = SCAN_PAIRS
    width = npair * LANES
    groups = HEAD_PAIRS // npair
    n_t = n_tok // tm
    row8 = tm // 8
    n_row8 = t // 8

    def tile_of(ti):
        return (n_t - 1 - ti) if reverse else ti

    def token_specs(w, cb):
        return [pl.BlockSpec((tm, w), lambda bb, p, ti: (bb * n_t + tile_of(ti), cb(p))),
                pl.BlockSpec((8, w), lambda bb, p, ti: (jnp.maximum((bb * n_t + tile_of(ti)) * row8 - 1, 0), cb(p))),
                pl.BlockSpec((8, w), lambda bb, p, ti: (jnp.minimum((bb * n_t + tile_of(ti) + 1) * row8, n_row8 - 1),
                                                        cb(p))),
                pl.BlockSpec((1, w), lambda bb, p, ti: (0, cb(p)))]

    rowvec = lambda: pl.BlockSpec((1, width), lambda bb, p, ti: (0, p))
    lora = lambda: pl.BlockSpec((LANES, width), lambda bb, p, ti: (0, p))
    const = lambda shape: pl.BlockSpec(shape, lambda bb, p, ti: (0, 0))
    state = lambda: pl.BlockSpec((None, npair, LANES, LANES), lambda bb, p, ti: (bb, p, 0, 0))
    y_spec = lambda: pl.BlockSpec((tm, width), lambda bb, p, ti: (bb * n_t + tile_of(ti), p))

    lora_blk = 3 * RWKV_WIDTH // LANES
    in_specs, args = [], []
    for name, w, cb in (("r", width, lambda p: p), ("k", width, lambda p: groups + p),
                        ("v", width, lambda p: 2 * groups + p), ("wa", LANES, lambda p: lora_blk),
                        ("gd", LANES, lambda p: lora_blk + 1)):
        if name == "gd" and not finalize:
            continue
        in_specs += token_specs(w, cb)
        args += [proj_rw, proj_rw, proj_rw, prm["mu"]]
    d, o = prm["dirs"][direction], prm["dirs"][1 - direction]
    in_specs += [rowvec(), rowvec(), rowvec(), lora(), rowvec(), lora()]
    args += [prm["k_k"], prm["k_a"], d["w0"], d["w_up"], d["a0"], d["a_up"]]
    if finalize:
        in_specs += [rowvec(), rowvec(), lora(), lora(), rowvec(), rowvec(), y_spec()]
        args += [prm["r_k"], o["a0"], o["a_up"], prm["g_up"], prm["gn_g"], prm["gn_b"], y_other]
    in_specs += [const(prm["ones_bd"].shape), const((CHUNK, CHUNK))]
    args += [prm["ones_bd"], prm["tri"][direction]]
    if not zero_init:
        in_specs.append(state())
        args.append(s0_bd)
    out_specs = [y_spec()]
    out_shape = [jax.ShapeDtypeStruct((t, RWKV_WIDTH), BF16 if finalize else F32)]
    if emit_state:
        out_specs.append(state())
        out_shape.append(jax.ShapeDtypeStruct((batch, HEAD_PAIRS, LANES, LANES), F32))
    out = pl.pallas_call(
        functools.partial(_scan_kernel, tm=tm, npair=npair, reverse=reverse, zero_init=zero_init,
                          emit_state=emit_state, finalize=finalize),
        grid=(batch, groups, n_t),
        in_specs=in_specs, out_specs=out_specs, out_shape=out_shape,
        scratch_shapes=[pltpu.VMEM((npair, LANES, LANES), F32)],
        compiler_params=_cp(("arbitrary", "arbitrary", "arbitrary")),
        name="rwkv_scan_bwd" if reverse else "rwkv_scan_fwd",
    )(*args)
    return out if emit_state else (out[0], None)


def _out_kernel(att_ref, rw_ref, x_ref, g_ref, wa_ref, wr_ref, lg_ref, lb_ref, o_ref):
    f = _dot(att_ref[...], wa_ref[...]) + _dot(rw_ref[...], wr_ref[...])
    o_ref[...] = _layer_norm(ALPHA * x_ref[...] + g_ref[...] * f, lg_ref[...], lb_ref[...])


def _out_proj(att, rw, x2, mod4, seq0, n_seq_tokens, w_att, w_rw, ln_g, ln_b):
    t, tm = x2.shape[0], 256
    per_seq = n_seq_tokens // tm
    seq_of = lambda i: seq0 + i // per_seq
    half = D_MODEL // 2
    const = lambda i: (0, 0)
    return pl.pallas_call(
        _out_kernel,
        grid=(t // tm,),
        in_specs=[pl.BlockSpec((tm, half), lambda i: (i, 0)), pl.BlockSpec((tm, half), lambda i: (i, 0)),
                  pl.BlockSpec((tm, D_MODEL), lambda i: (i, 0)), _mod_spec(2, seq_of),
                  pl.BlockSpec((half, D_MODEL), const), pl.BlockSpec((half, D_MODEL), const),
                  pl.BlockSpec((1, D_MODEL), const), pl.BlockSpec((1, D_MODEL), const)],
        out_specs=pl.BlockSpec((tm, D_MODEL), lambda i: (i, 0)),
        out_shape=jax.ShapeDtypeStruct((t, D_MODEL), F32),
        compiler_params=_cp(("arbitrary",)),
        name="out_proj",
    )(att, rw, x2, mod4, w_att, w_rw, ln_g.reshape(1, D_MODEL), ln_b.reshape(1, D_MODEL))


def _ffn_kernel(x_ref, sh_ref, sc_ref, g_ref, wg_ref, wu_ref, wd_ref, lg_ref, lb_ref, o_ref, h_scr, acc_scr):
    kf = pl.program_id(1)

    @pl.when(kf == 0)
    def _():
        h_scr[...] = (x_ref[...] * (1.0 + sc_ref[...]) + sh_ref[...]).astype(BF16)
        acc_scr[...] = jnp.zeros_like(acc_scr)

    h = h_scr[...]
    gate = _dot(h, wg_ref[...])
    act = (gate * _sigmoid(gate) * _dot(h, wu_ref[...])).astype(BF16)
    acc_scr[...] += _dot(act, wd_ref[...])

    @pl.when(kf == pl.num_programs(1) - 1)
    def _():
        o_ref[...] = _layer_norm(ALPHA * x_ref[...] + g_ref[...] * acc_scr[...], lg_ref[...], lb_ref[...])


def _ffn(x1, mod4, seq0, n_seq_tokens, w_gate, w_up, w_down, ln_g, ln_b):
    t, tm, tf = x1.shape[0], 512, 512
    tm = min(tm, n_seq_tokens)
    per_seq = n_seq_tokens // tm
    seq_of = lambda i, kf: seq0 + i // per_seq
    const = lambda i, kf: (0, 0)
    return pl.pallas_call(
        _ffn_kernel,
        grid=(t // tm, D_FF // tf),
        in_specs=[pl.BlockSpec((tm, D_MODEL), lambda i, kf: (i, 0)),
                  _mod_spec(3, seq_of), _mod_spec(4, seq_of), _mod_spec(5, seq_of),
                  pl.BlockSpec((D_MODEL, tf), lambda i, kf: (0, kf)),
                  pl.BlockSpec((D_MODEL, tf), lambda i, kf: (0, kf)),
                  pl.BlockSpec((tf, D_MODEL), lambda i, kf: (kf, 0)),
                  pl.BlockSpec((1, D_MODEL), const), pl.BlockSpec((1, D_MODEL), const)],
        out_specs=pl.BlockSpec((tm, D_MODEL), lambda i, kf: (i, 0)),
        out_shape=jax.ShapeDtypeStruct((t, D_MODEL), F32),
        scratch_shapes=[pltpu.VMEM((tm, D_MODEL), BF16), pltpu.VMEM((tm, D_MODEL), F32)],
        compiler_params=_cp(("arbitrary", "arbitrary")),
        name="ffn",
    )(x1, mod4, mod4, mod4, w_gate, w_up, w_down, ln_g.reshape(1, D_MODEL), ln_b.reshape(1, D_MODEL))


def _rope_tables(n_tokens):
    rows = n_tokens // GRID_W
    row = jnp.repeat(jnp.arange(rows), GRID_W).astype(F32)
    col = jnp.tile(jnp.arange(GRID_W), rows).astype(F32)
    freqs = ROPE_THETA ** (-jnp.arange(ROPE_AXIS_FREQS, dtype=F32) / ROPE_AXIS_FREQS)
    ang = jnp.concatenate([row[:, None] * freqs, col[:, None] * freqs], -1)
    cos, sin = jnp.cos(ang), jnp.sin(ang)
    cos2 = jnp.concatenate([cos, cos], -1)
    sin2 = jnp.concatenate([-sin, sin], -1)
    pad_lo = jnp.zeros((n_tokens, QK_NOPE), F32)
    pad_hi = jnp.zeros((n_tokens, QK_PAD - QK_NOPE - QK_ROPE), F32)
    q_cos = jnp.concatenate([pad_lo + 1.0, cos2, pad_hi], -1)
    q_sin = jnp.concatenate([pad_lo, sin2, pad_hi], -1)
    k_tab = jnp.concatenate([cos2, sin2], -1)
    return q_cos, q_sin, k_tab


def _swap_halves(w):
    half = w.shape[-1] // 2
    return jnp.concatenate([w[..., half:], w[..., :half]], -1)


def _block_diag_state(s):
    bsz = s.shape[0]
    s = s.reshape(bsz, HEAD_PAIRS, 2, RWKV_HEAD, RWKV_HEAD)
    z = jnp.zeros_like(s[:, :, 0])
    top = jnp.concatenate([s[:, :, 0], z], -1)
    bot = jnp.concatenate([z, s[:, :, 1]], -1)
    return jnp.concatenate([top, bot], -2)


def _unblock_diag_state(s_bd):
    bsz = s_bd.shape[0]
    s = jnp.stack([s_bd[:, :, :RWKV_HEAD, :RWKV_HEAD], s_bd[:, :, RWKV_HEAD:, RWKV_HEAD:]], 2)
    return s.reshape(bsz, RWKV_HEADS, RWKV_HEAD, RWKV_HEAD)


def _layer(x2, batch, n_tok, mod4, seq0, per_seq_mod, wts, rope_tabs, cache, states, is_context):
    mod_tokens = n_tok if per_seq_mod else 0
    proj_mla, proj_rw = _in_proj(x2, mod4, seq0, mod_tokens, wts["w_mla"], wts["w_rw"])
    q = _q_proj(proj_mla, n_tok, wts["q_norm_g"], wts["wq_full"], wts["wq_sw"],
                None if rope_tabs is None else rope_tabs[:2])
    kv = _kv_proj(proj_mla, OFF_KV // KV_LORA, proj_mla, OFF_KR // LANES, LANES, n_tok, wts["kv_norm_g"],
                  None if rope_tabs is None else rope_tabs[2], wts["wk_full"], wts["w_uv"], wts["e_mat"], is_context)
    kc = vc = None
    if cache is not None:
        kc, vc = _kv_proj(cache[0], 0, cache[1], 0, QK_ROPE, cache[0].shape[0] // batch, None, None,
                          wts["wk_full"], wts["w_uv"], wts["e_mat"], False)
    att = _attention(q, kv[0], kv[1], kc, vc, batch, n_tok)
    s_f = None if states is None else _block_diag_state(states[0])
    s_b = None if states is None else _block_diag_state(states[1])
    y_f, new_f = _scan(proj_rw, batch, n_tok, wts["rwkv"], 0, s_f, is_context, None)
    rw, new_b = _scan(proj_rw, batch, n_tok, wts["rwkv"], 1, s_b, is_context, y_f)
    seq_tokens = n_tok if per_seq_mod else x2.shape[0]
    x1 = _out_proj(att, rw, x2, mod4, seq0, seq_tokens, wts["w_out_att"], wts["w_out_rw"], wts["ln1_g"], wts["ln1_b"])
    y = _ffn(x1, mod4, seq0, seq_tokens, wts["w_ffn_gate"], wts["w_ffn_up"], wts["w_ffn_down"],
             wts["ln2_g"], wts["ln2_b"])
    return y, kv, new_f, new_b


def kernel(x_prompt, x_sample, cache_ckv, cache_krope, state_wkv_fwd, state_wkv_bwd, c, c_ctx, w_mod, b_mod, w_in, q_norm_g, kv_norm_g, w_uq, w_uk, w_uv, tok_shift_mu, w0_fwd, w_up_fwd, a0_fwd, a_up_fwd, w0_bwd, w_up_bwd, a0_bwd, a_up_bwd, g_up, k_k, k_a, r_k, gn_g, gn_b, w_out, ln1_g, ln1_b, w_ffn_gate, w_ffn_up, w_ffn_down, ln2_g, ln2_b):
    batch, seq = x_prompt.shape[:2]
    dec_batch, dec_seq = x_sample.shape[:2]
    layer = 0

    mod_rows = 16
    cond = jnp.concatenate([c, c_ctx[None, :], jnp.zeros((mod_rows - dec_batch - 1, D_MODEL), F32)], 0)
    mod4 = _modulation(cond, w_mod[layer], b_mod[layer]).reshape(mod_rows, 6, 1, D_MODEL)

    wi = w_in[layer]
    kr_w = wi[:, OFF_KR:OFF_RW]
    uq = w_uq[layer].reshape(Q_LORA, MLA_HEADS, QK_NOPE + QK_ROPE)
    uq_pad = jnp.zeros((Q_LORA, MLA_HEADS, QK_PAD - QK_NOPE - QK_ROPE), F32)
    wq_full = jnp.concatenate([uq, uq_pad], -1).reshape(Q_LORA, MLA_HEADS * QK_PAD)
    wq_sw = jnp.concatenate([jnp.zeros((Q_LORA, MLA_HEADS, QK_NOPE), F32), _swap_halves(uq[..., QK_NOPE:]), uq_pad],
                            -1).reshape(Q_LORA, MLA_HEADS * QK_PAD)
    uk = w_uk[layer].reshape(KV_LORA, MLA_HEADS, QK_NOPE)
    wk_full = jnp.concatenate([uk, jnp.zeros((KV_LORA, MLA_HEADS, QK_PAD - QK_NOPE), F32)], -1)
    e_head = jnp.concatenate([jnp.zeros((QK_ROPE, QK_NOPE), F32), jnp.eye(QK_ROPE, dtype=F32),
                              jnp.zeros((QK_ROPE, QK_PAD - QK_NOPE - QK_ROPE), F32)], -1)
    lora_pad = jnp.zeros((DECAY_LORA, RWKV_WIDTH), F32)
    row = lambda vec: vec[layer].reshape(1, -1)
    head_of = jnp.arange(2 * LANES) // RWKV_HEAD
    tri_lo = jnp.tril(jnp.ones((CHUNK, CHUNK), F32))
    rwkv = {
        "mu": row(tok_shift_mu), "k_k": row(k_k), "k_a": row(k_a), "r_k": row(r_k),
        "gn_g": row(gn_g), "gn_b": row(gn_b), "g_up": g_up[layer].astype(BF16),
        "dirs": [
            {"w0": row(w0_fwd), "a0": row(a0_fwd),
             "w_up": jnp.concatenate([w_up_fwd[layer], lora_pad], 0).astype(BF16),
             "a_up": jnp.concatenate([lora_pad, a_up_fwd[layer]], 0).astype(BF16)},
            {"w0": row(w0_bwd), "a0": row(a0_bwd),
             "w_up": jnp.concatenate([w_up_bwd[layer], lora_pad], 0).astype(BF16),
             "a_up": jnp.concatenate([lora_pad, a_up_bwd[layer]], 0).astype(BF16)}],
        "ones_bd": (head_of[:, None] == head_of[None, :]).astype(BF16),
        "tri": [tri_lo.astype(BF16), tri_lo.T.astype(BF16)],
    }
    wts = {
        "w_mla": jnp.concatenate([wi[:, :OFF_RW], _swap_halves(kr_w)], -1).astype(BF16),
        "w_rw": wi[:, OFF_RW:].astype(BF16),
        "q_norm_g": q_norm_g[layer], "kv_norm_g": kv_norm_g[layer],
        "wq_full": wq_full.astype(BF16), "wq_sw": wq_sw.astype(BF16),
        "wk_full": wk_full.reshape(KV_LORA, MLA_HEADS * QK_PAD).astype(BF16),
        "w_uv": w_uv[layer].astype(BF16),
        "e_mat": jnp.tile(e_head, (1, MLA_HEADS)).astype(BF16),
        "rwkv": rwkv,
        "w_out_att": w_out[layer][:MLA_HEADS * V_HEAD].astype(BF16),
        "w_out_rw": w_out[layer][MLA_HEADS * V_HEAD:].astype(BF16),
        "ln1_g": ln1_g[layer], "ln1_b": ln1_b[layer],
        "w_ffn_gate": w_ffn_gate[layer].astype(BF16), "w_ffn_up": w_ffn_up[layer].astype(BF16),
        "w_ffn_down": w_ffn_down[layer].astype(BF16),
        "ln2_g": ln2_g[layer], "ln2_b": ln2_b[layer],
    }

    y_prompt, kv_ctx, s_f, s_b = _layer(
        x_prompt.reshape(batch * seq, D_MODEL), batch, seq, mod4, dec_batch, False, wts, None, None, None, True)
    past = cache_ckv.shape[2]
    cache = (cache_ckv[:, layer].reshape(dec_batch * past, KV_LORA),
             cache_krope[:, layer].reshape(dec_batch * past, QK_ROPE))
    y_sample, _, _, _ = _layer(
        x_sample.reshape(dec_batch * dec_seq, D_MODEL), dec_batch, dec_seq, mod4, 0, True, wts,
        _rope_tables(dec_seq), cache, (state_wkv_fwd[:, layer], state_wkv_bwd[:, layer]), False)

    return (y_prompt.reshape(batch, seq, D_MODEL),
            y_sample.reshape(dec_batch, dec_seq, D_MODEL),
            kv_ctx[2].reshape(batch, 1, seq, KV_LORA),
            kv_ctx[3].reshape(batch, 1, seq, QK_ROPE),
            _unblock_diag_state(s_f)[:, None],
            _unblock_diag_state(s_b)[:, None])
```

```python
import functools
import math

import jax
import jax.numpy as jnp
from jax import lax
from jax.experimental import pallas as pl
from jax.experimental.pallas import tpu as pltpu

F32 = jnp.float32
BF16 = jnp.bfloat16

D_MODEL = 2048
GRID_W = 64
MLA_HEADS = 8
QK_NOPE = 128
QK_ROPE = 64
V_HEAD = 128
Q_LORA = 512
KV_LORA = 256
ROPE_AXIS_FREQS = QK_ROPE // 4
ROPE_THETA = 10000.0
RWKV_HEADS = 16
RWKV_HEAD = 64
RWKV_WIDTH = RWKV_HEADS * RWKV_HEAD
DECAY_LORA = 64
ICLR_LORA = 64
GATE_LORA = 128
OFF_KV = Q_LORA
OFF_KR = OFF_KV + KV_LORA
OFF_RW = OFF_KR + QK_ROPE
RW_COLS = 3 * RWKV_WIDTH + DECAY_LORA + ICLR_LORA + GATE_LORA
D_FF = 5632
LN_EPS = 1e-5
RMS_EPS = 1e-6
GN_EPS = 64e-5
DEPTH = 1
ALPHA = (2.0 * DEPTH) ** 0.25

LANES = 128
QK_PAD = 2 * LANES
MLA_COLS = OFF_RW + QK_ROPE
CHUNK = 64
HEAD_PAIRS = RWKV_HEADS // 2
ATTN_TQ = 512
ATTN_HEADS = 4
FFN_TM = 512
FFN_TF = 512
SCAN_TM = 512
SCAN_GROUP = 2
SCAN_PAIRS = 8
VMEM_LIMIT = 56 * 1024 * 1024


def _cp(sem):
    return pltpu.CompilerParams(dimension_semantics=sem, vmem_limit_bytes=VMEM_LIMIT)


def _dot(a, b):
    return jnp.dot(a, b, preferred_element_type=F32)


def _dot_nt(a, b):
    return lax.dot_general(a, b, (((1,), (1,)), ((), ())), preferred_element_type=F32)


def _dot_tn(a, b):
    return lax.dot_general(a, b, (((0,), (0,)), ((), ())), preferred_element_type=F32)


def _sigmoid(x):
    return 1.0 / (1.0 + jnp.exp(-x))


def _layer_norm(h, g, b):
    mu = jnp.mean(h, -1, keepdims=True)
    d = h - mu
    var = jnp.mean(d * d, -1, keepdims=True)
    return d * lax.rsqrt(var + LN_EPS) * g + b


def _split3(x):
    hi = x.astype(BF16)
    r1 = x - hi.astype(F32)
    mid = r1.astype(BF16)
    lo = (r1 - mid.astype(F32)).astype(BF16)
    return hi, mid, lo


def _dot2_right(x, m):
    hi = x.astype(BF16)
    lo = (x - hi.astype(F32)).astype(BF16)
    return _dot(hi, m) + _dot(lo, m)


def _dot3_left(m, x):
    hi, mid, lo = _split3(x)
    return _dot(m, hi) + _dot(m, mid) + _dot(m, lo)


def _mod_kernel(c_ref, w_ref, b_ref, o_ref):
    c = c_ref[...]
    s = (c * _sigmoid(c)).astype(BF16)
    o_ref[...] = _dot(s, w_ref[...].astype(BF16)) + b_ref[...]


def _modulation(cond, w_mod, b_mod):
    rows, tn = cond.shape[0], 1024
    n = w_mod.shape[1]
    return pl.pallas_call(
        _mod_kernel,
        grid=(n // tn,),
        in_specs=[pl.BlockSpec((rows, D_MODEL), lambda j: (0, 0)),
                  pl.BlockSpec((D_MODEL, tn), lambda j: (0, j)),
                  pl.BlockSpec((1, tn), lambda j: (0, j))],
        out_specs=pl.BlockSpec((rows, tn), lambda j: (0, j)),
        out_shape=jax.ShapeDtypeStruct((rows, n), F32),
        compiler_params=_cp(("arbitrary",)),
        name="modulation",
    )(cond, w_mod, b_mod.reshape(1, n))


def _mod_spec(which, seq_of):
    return pl.BlockSpec((None, None, 1, D_MODEL), lambda *g: (seq_of(*g), which, 0, 0))


def _in_kernel(x_ref, sh_ref, sc_ref, wm_ref, wr_ref, om_ref, or_ref):
    xm = (x_ref[...] * (1.0 + sc_ref[...]) + sh_ref[...]).astype(BF16)
    om_ref[...] = _dot(xm, wm_ref[...])
    or_ref[...] = _dot(xm, wr_ref[...])


def _in_proj(x2, mod4, seq0, n_seq_tokens, w_mla, w_rw):
    t, tm = x2.shape[0], 512
    per_seq = n_seq_tokens // tm
    seq_of = (lambda i: seq0) if per_seq == 0 else (lambda i: seq0 + i // per_seq)
    resident = pl.Buffered(1)
    return pl.pallas_call(
        _in_kernel,
        grid=(t // tm,),
        in_specs=[pl.BlockSpec((tm, D_MODEL), lambda i: (i, 0)),
                  _mod_spec(0, seq_of), _mod_spec(1, seq_of),
                  pl.BlockSpec((D_MODEL, MLA_COLS), lambda i: (0, 0), pipeline_mode=resident),
                  pl.BlockSpec((D_MODEL, RW_COLS), lambda i: (0, 0), pipeline_mode=resident)],
        out_specs=[pl.BlockSpec((tm, MLA_COLS), lambda i: (i, 0)),
                   pl.BlockSpec((tm, RW_COLS), lambda i: (i, 0))],
        out_shape=[jax.ShapeDtypeStruct((t, MLA_COLS), F32),
                   jax.ShapeDtypeStruct((t, RW_COLS), F32)],
        compiler_params=_cp(("arbitrary",)),
        name="in_proj",
    )(x2, mod4, mod4, w_mla, w_rw)


def _q_kernel(*refs, rope):
    if rope:
        qd_ref, g_ref, w_ref, wsw_ref, c_ref, s_ref, o_ref = refs
    else:
        qd_ref, g_ref, w_ref, o_ref = refs
    x = qd_ref[...]
    qn = (x * lax.rsqrt(jnp.mean(x * x, -1, keepdims=True) + RMS_EPS) * g_ref[...]).astype(BF16)
    q = _dot(qn, w_ref[...])
    if rope:
        c = jnp.concatenate([c_ref[...]] * MLA_HEADS, axis=1)
        s = jnp.concatenate([s_ref[...]] * MLA_HEADS, axis=1)
        q = q * c + _dot(qn, wsw_ref[...]) * s
    o_ref[...] = q.astype(BF16)


def _q_proj(proj_mla, n_seq_tokens, q_norm_g, wq_full, wq_sw, rope_tabs):
    t, tm = proj_mla.shape[0], 256
    per_seq = n_seq_tokens // tm
    rope = rope_tabs is not None
    width = MLA_HEADS * QK_PAD
    const = lambda i: (0, 0)
    in_specs = [pl.BlockSpec((tm, Q_LORA), lambda i: (i, 0)),
                pl.BlockSpec((1, Q_LORA), const),
                pl.BlockSpec((Q_LORA, width), const)]
    args = [proj_mla, q_norm_g.reshape(1, Q_LORA), wq_full]
    if rope:
        in_specs += [pl.BlockSpec((Q_LORA, width), const),
                     pl.BlockSpec((tm, QK_PAD), lambda i: (i % per_seq, 0)),
                     pl.BlockSpec((tm, QK_PAD), lambda i: (i % per_seq, 0))]
        args += [wq_sw, rope_tabs[0], rope_tabs[1]]
    return pl.pallas_call(
        functools.partial(_q_kernel, rope=rope),
        grid=(t // tm,),
        in_specs=in_specs,
        out_specs=pl.BlockSpec((tm, width), lambda i: (i, 0)),
        out_shape=jax.ShapeDtypeStruct((t, width), BF16),
        compiler_params=_cp(("arbitrary",)),
        name="q_proj",
    )(*args)


def _kv_kernel(*refs, norm, rope, emit_new):
    it = iter(refs)
    ckv_ref, kr_ref = next(it), next(it)
    g_ref = next(it) if norm else None
    cs_ref = next(it) if rope else None
    wk_ref, wv_ref, e_ref = next(it), next(it), next(it)
    k_out, v_out = next(it), next(it)
    x = ckv_ref[...]
    if norm:
        x = x * lax.rsqrt(jnp.mean(x * x, -1, keepdims=True) + RMS_EPS) * g_ref[...]
    xb = x.astype(BF16)
    kr = kr_ref[...]
    if rope:
        y = kr * cs_ref[...]
        kr = y + pltpu.roll(y, QK_ROPE, 1)
    kr = kr[:, :QK_ROPE]
    k_out[...] = (_dot(xb, wk_ref[...]) + _dot(kr.astype(BF16), e_ref[...])).T.astype(BF16)
    v_out[...] = _dot(xb, wv_ref[...]).astype(BF16)
    if emit_new:
        nc_out, nk_out = next(it), next(it)
        nc_out[...] = x
        nk_out[...] = kr


def _kv_proj(ckv_src, ckv_blk, kr_src, kr_blk, kr_w, n_seq_tokens, kv_norm_g, cs_tab, wk_full, wv, e_mat, emit_new):
    t, tm = ckv_src.shape[0], 256
    per_seq = n_seq_tokens // tm
    norm, rope = kv_norm_g is not None, cs_tab is not None
    kw, vw = MLA_HEADS * QK_PAD, MLA_HEADS * V_HEAD
    const = lambda i: (0, 0)
    in_specs = [pl.BlockSpec((tm, KV_LORA), lambda i: (i, ckv_blk)),
                pl.BlockSpec((tm, kr_w), lambda i: (i, kr_blk))]
    args = [ckv_src, kr_src]
    if norm:
        in_specs.append(pl.BlockSpec((1, KV_LORA), const))
        args.append(kv_norm_g.reshape(1, KV_LORA))
    if rope:
        in_specs.append(pl.BlockSpec((tm, LANES), lambda i: (i % per_seq, 0)))
        args.append(cs_tab)
    in_specs += [pl.BlockSpec((KV_LORA, kw), const), pl.BlockSpec((KV_LORA, vw), const),
                 pl.BlockSpec((QK_ROPE, kw), const)]
    args += [wk_full, wv, e_mat]
    out_specs = [pl.BlockSpec((kw, tm), lambda i: (i // per_seq, i % per_seq)),
                 pl.BlockSpec((tm, vw), lambda i: (i, 0))]
    out_shape = [jax.ShapeDtypeStruct((t // n_seq_tokens * kw, n_seq_tokens), BF16),
                 jax.ShapeDtypeStruct((t, vw), BF16)]
    if emit_new:
        out_specs += [pl.BlockSpec((tm, KV_LORA), lambda i: (i, 0)), pl.BlockSpec((tm, QK_ROPE), lambda i: (i, 0))]
        out_shape += [jax.ShapeDtypeStruct((t, KV_LORA), F32), jax.ShapeDtypeStruct((t, QK_ROPE), F32)]
    return pl.pallas_call(
        functools.partial(_kv_kernel, norm=norm, rope=rope, emit_new=emit_new),
        grid=(t // tm,),
        in_specs=in_specs, out_specs=out_specs, out_shape=out_shape,
        compiler_params=_cp(("arbitrary",)),
        name="kv_proj",
    )(*args)


def _attn_kernel(*refs, cache, heads, exp2_scale):
    if cache:
        q_ref, k_ref, v_ref, kc_ref, vc_ref, o_ref = refs
    else:
        q_ref, k_ref, v_ref, o_ref = refs
    hs = range(heads)
    qk = lambda h: slice(h * QK_PAD, (h + 1) * QK_PAD)
    hv = lambda h: slice(h * V_HEAD, (h + 1) * V_HEAD)
    q = [q_ref[:, qk(h)] for h in hs]
    s = [_dot(q[h], k_ref[qk(h), :]) for h in hs]
    m = [jnp.max(s[h], -1, keepdims=True) for h in hs]
    if cache:
        sc = [_dot(q[h], kc_ref[qk(h), :]) for h in hs]
        m = [jnp.maximum(m[h], jnp.max(sc[h], -1, keepdims=True)) for h in hs]
    p = [jnp.exp2((s[h] - m[h]) * exp2_scale) for h in hs]
    l = [jnp.sum(p[h], -1, keepdims=True) for h in hs]
    acc = [_dot(p[h].astype(BF16), v_ref[:, hv(h)]) for h in hs]
    if cache:
        pc = [jnp.exp2((sc[h] - m[h]) * exp2_scale) for h in hs]
        l = [l[h] + jnp.sum(pc[h], -1, keepdims=True) for h in hs]
        acc = [acc[h] + _dot(pc[h].astype(BF16), vc_ref[:, hv(h)]) for h in hs]
    o_ref[...] = jnp.concatenate([acc[h] * (1.0 / l[h]) for h in hs], axis=1).astype(BF16)


def _attention(q, kt, v, kct, vc, batch, n_tok):
    tq, heads = min(ATTN_TQ, n_tok), ATTN_HEADS
    n_q = n_tok // tq
    groups = MLA_HEADS // heads
    cache = kct is not None
    in_specs = [pl.BlockSpec((tq, heads * QK_PAD), lambda b, h, i: (b * n_q + i, h)),
                pl.BlockSpec((heads * QK_PAD, n_tok), lambda b, h, i: (b * groups + h, 0)),
                pl.BlockSpec((n_tok, heads * V_HEAD), lambda b, h, i: (b, h))]
    args = [q, kt, v]
    if cache:
        n_c = vc.shape[0] // batch
        in_specs += [pl.BlockSpec((heads * QK_PAD, n_c), lambda b, h, i: (b * groups + h, 0)),
                     pl.BlockSpec((n_c, heads * V_HEAD), lambda b, h, i: (b, h))]
        args += [kct, vc]
    scale = 1.0 / math.sqrt(QK_NOPE + QK_ROPE)
    return pl.pallas_call(
        functools.partial(_attn_kernel, cache=cache, heads=heads, exp2_scale=scale * math.log2(math.e)),
        grid=(batch, groups, n_q),
        in_specs=in_specs,
        out_specs=pl.BlockSpec((tq, heads * V_HEAD), lambda b, h, i: (b * n_q + i, h)),
        out_shape=jax.ShapeDtypeStruct((batch * n_tok, MLA_HEADS * V_HEAD), BF16),
        compiler_params=_cp(("arbitrary", "arbitrary", "arbitrary")),
        name="attention",
    )(*args)


def _pair_blockdiag(y, lane_lo):
    zero = jnp.zeros_like(y)
    return jnp.concatenate([jnp.where(lane_lo, y, zero), jnp.where(lane_lo, zero, y)], axis=0)


def _scan_kernel(*refs, tm, npair, reverse, zero_init, emit_state, finalize):
    it = iter(refs)
    blocks = {}
    cols = ("r", "k", "v", "wa") + (("gd",) if finalize else ())
    for name in cols:
        blocks[name] = (next(it), next(it), next(it), next(it))
    kk_ref, ka_ref, w0_ref, wup_ref, a0_ref, aup_ref = (next(it) for _ in range(6))
    if finalize:
        rk_ref, a0o_ref, aupo_ref, gup_ref, gng_ref, gnb_ref, yo_ref = (next(it) for _ in range(7))
    ones_ref, tri_ref = next(it), next(it)
    s0_ref = None if zero_init else next(it)
    y_ref = next(it)
    sout_ref = next(it) if emit_state else None
    s_scr = next(it)

    t_idx = pl.program_id(2)
    n_t = pl.num_programs(2)
    tile = (n_t - 1 - t_idx) if reverse else t_idx
    first_tok = tile == 0
    last_tok = tile == n_t - 1

    @pl.when(t_idx == 0)
    def _():
        if zero_init:
            s_scr[...] = jnp.zeros_like(s_scr)
        else:
            s_scr[...] = s0_ref[...]

    def shifted(name):
        main, prv, nxt, mu = blocks[name]
        u = main[...]
        row = lax.broadcasted_iota(jnp.int32, u.shape, 0)
        p_row = jnp.where(first_tok, 0.0, prv[7:8, :])
        n_row = jnp.where(last_tok, 0.0, nxt[0:1, :])
        up = jnp.where(row == 0, p_row, pltpu.roll(u, 1, 0))
        un = jnp.where(row == tm - 1, n_row, pltpu.roll(u, tm - 1, 0))
        return u + mu[...] * (0.5 * (up + un) - u)

    ones_bd = ones_ref[...]

    seg_w = ones_bd.shape[0]

    def seg(x):
        return jnp.concatenate(
            [_dot2_right(x[:, j * seg_w:(j + 1) * seg_w], ones_bd) for j in range(x.shape[1] // seg_w)], axis=1)

    r, k, v, wa = shifted("r"), shifted("k"), shifted("v"), shifted("wa")
    wab = wa.astype(BF16)
    kk = k * kk_ref[...]
    kk = kk * lax.rsqrt(jnp.maximum(seg(kk * kk), 1e-24))
    ld = _sigmoid(w0_ref[...] + _dot(jnp.tanh(wa).astype(BF16), wup_ref[...])) * (-math.exp(-0.5))
    a = _sigmoid(a0_ref[...] + _dot(wab, aup_ref[...]))
    kd = k * (1.0 + (a - 1.0) * ka_ref[...])
    b = kk * a

    c_row = lax.broadcasted_iota(jnp.int32, (CHUNK, LANES), 0)
    c_lane = lax.broadcasted_iota(jnp.int32, (CHUNK, LANES), 1)
    c_src = c_lane & (CHUNK - 1)
    lane_lo = c_lane < CHUNK
    strict = (c_src > c_row) if reverse else (c_src < c_row)
    incl = (c_src >= c_row) if reverse else (c_src <= c_row)
    eye = (c_src == c_row).astype(F32)
    bd_row = lax.broadcasted_iota(jnp.int32, (LANES, LANES), 0)
    bd_lane = lax.broadcasted_iota(jnp.int32, (LANES, LANES), 1)
    bd_mask = jnp.right_shift(bd_row, 6) == jnp.right_shift(bd_lane, 6)
    tri = tri_ref[...]

    def nn(p, y):
        return _dot(p.astype(BF16), _pair_blockdiag(y.astype(BF16), lane_lo))

    n_chunks = tm // CHUNK
    order = list(range(n_chunks - 1, -1, -1) if reverse else range(n_chunks))
    cums = [_dot3_left(tri, ld[c * CHUNK:(c + 1) * CHUNK]) for c in range(n_chunks)]
    pairs = range(npair)
    strict2 = jnp.concatenate([strict, strict], axis=1)
    incl2 = jnp.concatenate([incl, incl], axis=1)
    st = {}

    def independent_stage(c):
        sl = slice(c * CHUNK, (c + 1) * CHUNK)
        for j in pairs:
            ln = slice(j * LANES, (j + 1) * LANES)
            r_c, v_c, kk_c, kd_c, b_c, ld_c = r[sl, ln], v[sl, ln], kk[sl, ln], kd[sl, ln], b[sl, ln], ld[sl, ln]
            cum = cums[c][:, ln]
            tot = cum[0:1] if reverse else cum[CHUNK - 1:CHUNK]
            e_neg = jnp.exp(-cum)
            e_tot = jnp.exp(tot - cum)
            a_hat = -kk_c * jnp.exp(cum - ld_c)
            r_hat = r_c * jnp.exp(cum)
            ar = jnp.concatenate([a_hat, r_hat], axis=0).astype(BF16)
            bk_hat = jnp.concatenate([_pair_blockdiag((b_c * e_neg).astype(BF16), lane_lo),
                                      _pair_blockdiag((kd_c * e_neg).astype(BF16), lane_lo)], axis=0)
            g = _dot_nt(ar, bk_hat)
            gs = jnp.where(strict2, g[:CHUNK], 0.0)
            lmat = gs[:, :LANES]
            st[j, c] = dict(
                ar=ar, v=v_c.astype(BF16), v_bd=_pair_blockdiag(v_c.astype(BF16), lane_lo),
                p=eye + lmat, q=lmat, akm=gs[:, LANES:],
                rbk=jnp.where(incl2, g[CHUNK:], 0.0).astype(BF16),
                bk=jnp.concatenate([b_c * e_tot, kd_c * e_tot], axis=0).astype(BF16),
                w=jnp.exp(tot))
        yield
        for j in pairs:
            st[j, c]["q"] = nn(st[j, c]["q"], st[j, c]["q"])
        yield
        for _ in range(int(math.log2(CHUNK)) - 2):
            for j in pairs:
                d = st[j, c]
                pq = nn(jnp.concatenate([d["p"], d["q"]], axis=0), d["q"])
                d["p"], d["q"] = d["p"] + pq[:CHUNK], pq[CHUNK:]
            yield
        for j in pairs:
            d = st[j, c]
            d["p"] = d["p"] + nn(d["p"], d["q"])
            d["akv"] = _dot(d["akm"].astype(BF16), d["v_bd"])
        yield

    s_cur = [s_scr[j] for j in pairs]
    y_cells = {}

    def dependent_stage(c):
        xs = [_dot_nt(st[j, c]["ar"], s_cur[j].astype(BF16)) for j in pairs]
        yield
        u = [nn(st[j, c]["p"], xs[j][:CHUNK] + st[j, c]["akv"]) for j in pairs]
        yield
        for j in pairs:
            d = st[j, c]
            y_cells[j, c] = xs[j][CHUNK:] + _dot(
                d["rbk"],
                jnp.concatenate([_pair_blockdiag(u[j].astype(BF16), lane_lo), d["v_bd"]], axis=0))
            upd = _dot_tn(jnp.concatenate([u[j].astype(BF16), d["v"]], axis=0), d["bk"])
            s_cur[j] = s_cur[j] * d["w"] + jnp.where(bd_mask, upd, 0.0)
        yield

    def run_side_by_side(*stages):
        live = list(stages)
        while live:
            for g in list(live):
                if next(g, StopIteration) is StopIteration:
                    live.remove(g)

    def in_sequence(stages):
        for g in stages:
            yield from g

    groups = [order[i:i + SCAN_GROUP] for i in range(0, n_chunks, SCAN_GROUP)]
    for gi in range(len(groups) + 1):
        stages = [independent_stage(c) for c in groups[gi]] if gi < len(groups) else []
        if gi > 0:
            stages.append(in_sequence([dependent_stage(c) for c in groups[gi - 1]]))
        run_side_by_side(*stages)
    s_new = s_cur
    for j in range(npair):
        s_scr[j] = s_new[j]
    y = jnp.concatenate(
        [jnp.concatenate([y_cells[j, c] for c in range(n_chunks)], axis=0) for j in range(npair)], axis=1)

    if emit_state:
        @pl.when(t_idx == n_t - 1)
        def _():
            for j in range(npair):
                sout_ref[j] = s_new[j]

    if not finalize:
        y_ref[...] = y
    else:
        rk = rk_ref[...]
        a_o = _sigmoid(a0o_ref[...] + _dot(wab, aupo_ref[...]))
        kd_o = k * (1.0 + (a_o - 1.0) * ka_ref[...])
        bonus = seg(r * rk * (kd + kd_o))
        gate = _dot(_sigmoid(shifted("gd")).astype(BF16), gup_ref[...])
        yt = y + yo_ref[...]
        inv_n = 1.0 / RWKV_HEAD
        d = yt - seg(yt) * inv_n
        var = seg(d * d) * inv_n
        yn = d * lax.rsqrt(var + GN_EPS) * gng_ref[...] + gnb_ref[...]
        y_ref[...] = ((yn + bonus * v) * gate).astype(BF16)


def _scan(proj_rw, batch, n_tok, prm, direction, s0_bd, emit_state, y_other):
    reverse = direction == 1
    finalize = y_other is not None
    zero_init = s0_bd is None
    t = batch * n_tok
    tm = min(SCAN_TM, n_tok)
    npair = SCAN_PAIRS
    width = npair * LANES
    groups = HEAD_PAIRS // npair
    n_t = n_tok // tm
    row8 = tm // 8
    n_row8 = t // 8

    def tile_of(ti):
        return (n_t - 1 - ti) if reverse else ti

    def token_specs(w, cb):
        return [pl.BlockSpec((tm, w), lambda bb, p, ti: (bb * n_t + tile_of(ti), cb(p))),
                pl.BlockSpec((8, w), lambda bb, p, ti: (jnp.maximum((bb * n_t + tile_of(ti)) * row8 - 1, 0), cb(p))),
                pl.BlockSpec((8, w), lambda bb, p, ti: (jnp.minimum((bb * n_t + tile_of(ti) + 1) * row8, n_row8 - 1),
                                                        cb(p))),
                pl.BlockSpec((1, w), lambda bb, p, ti: (0, cb(p)))]

    rowvec = lambda: pl.BlockSpec((1, width), lambda bb, p, ti: (0, p))
    lora = lambda: pl.BlockSpec((LANES, width), lambda bb, p, ti: (0, p))
    const = lambda shape: pl.BlockSpec(shape, lambda bb, p, ti: (0, 0))
    state = lambda: pl.BlockSpec((None, npair, LANES, LANES), lambda bb, p, ti: (bb, p, 0, 0))
    y_spec = lambda: pl.BlockSpec((tm, width), lambda bb, p, ti: (bb * n_t + tile_of(ti), p))

    lora_blk = 3 * RWKV_WIDTH // LANES
    in_specs, args = [], []
    for name, w, cb in (("r", width, lambda p: p), ("k", width, lambda p: groups + p),
                        ("v", width, lambda p: 2 * groups + p), ("wa", LANES, lambda p: lora_blk),
                        ("gd", LANES, lambda p: lora_blk + 1)):
        if name == "gd" and not finalize:
            continue
        in_specs += token_specs(w, cb)
        args += [proj_rw, proj_rw, proj_rw, prm["mu"]]
    d, o = prm["dirs"][direction], prm["dirs"][1 - direction]
    in_specs += [rowvec(), rowvec(), rowvec(), lora(), rowvec(), lora()]
    args += [prm["k_k"], prm["k_a"], d["w0"], d["w_up"], d["a0"], d["a_up"]]
    if finalize:
        in_specs += [rowvec(), rowvec(), lora(), lora(), rowvec(), rowvec(), y_spec()]
        args += [prm["r_k"], o["a0"], o["a_up"], prm["g_up"], prm["gn_g"], prm["gn_b"], y_other]
    in_specs += [const(prm["ones_bd"].shape), const((CHUNK, CHUNK))]
    args += [prm["ones_bd"], prm["tri"][direction]]
    if not zero_init:
        in_specs.append(state())
        args.append(s0_bd)
    out_specs = [y_spec()]
    out_shape = [jax.ShapeDtypeStruct((t, RWKV_WIDTH), BF16 if finalize else F32)]
    if emit_state:
        out_specs.append(state())
        out_shape.append(jax.ShapeDtypeStruct((batch, HEAD_PAIRS, LANES, LANES), F32))
    out = pl.pallas_call(
        functools.partial(_scan_kernel, tm=tm, npair=npair, reverse=reverse, zero_init=zero_init,
                          emit_state=emit_state, finalize=finalize),
        grid=(batch, groups, n_t),
        in_specs=in_specs, out_specs=out_specs, out_shape=out_shape,
        scratch_shapes=[pltpu.VMEM((npair, LANES, LANES), F32)],
        compiler_params=_cp(("arbitrary", "arbitrary", "arbitrary")),
        name="rwkv_scan_bwd" if reverse else "rwkv_scan_fwd",
    )(*args)
    return out if emit_state else (out[0], None)


def _out_kernel(att_ref, rw_ref, x_ref, g_ref, wa_ref, wr_ref, lg_ref, lb_ref, o_ref):
    f = _dot(att_ref[...], wa_ref[...]) + _dot(rw_ref[...], wr_ref[...])
    o_ref[...] = _layer_norm(ALPHA * x_ref[...] + g_ref[...] * f, lg_ref[...], lb_ref[...])


def _out_proj(att, rw, x2, mod4, seq0, n_seq_tokens, w_att, w_rw, ln_g, ln_b):
    t, tm = x2.shape[0], 256
    per_seq = n_seq_tokens // tm
    seq_of = lambda i: seq0 + i // per_seq
    half = D_MODEL // 2
    const = lambda i: (0, 0)
    return pl.pallas_call(
        _out_kernel,
        grid=(t // tm,),
        in_specs=[pl.BlockSpec((tm, half), lambda i: (i, 0)), pl.BlockSpec((tm, half), lambda i: (i, 0)),
                  pl.BlockSpec((tm, D_MODEL), lambda i: (i, 0)), _mod_spec(2, seq_of),
                  pl.BlockSpec((half, D_MODEL), const), pl.BlockSpec((half, D_MODEL), const),
                  pl.BlockSpec((1, D_MODEL), const), pl.BlockSpec((1, D_MODEL), const)],
        out_specs=pl.BlockSpec((tm, D_MODEL), lambda i: (i, 0)),
        out_shape=jax.ShapeDtypeStruct((t, D_MODEL), F32),
        compiler_params=_cp(("arbitrary",)),
        name="out_proj",
    )(att, rw, x2, mod4, w_att, w_rw, ln_g.reshape(1, D_MODEL), ln_b.reshape(1, D_MODEL))


def _ffn_kernel(x_ref, sh_ref, sc_ref, g_ref, wg_ref, wu_ref, wd_ref, lg_ref, lb_ref, o_ref, h_scr, acc_scr):
    kf = pl.program_id(1)

    @pl.when(kf == 0)
    def _():
        h_scr[...] = (x_ref[...] * (1.0 + sc_ref[...]) + sh_ref[...]).astype(BF16)
        acc_scr[...] = jnp.zeros_like(acc_scr)

    h = h_scr[...]
    gate = _dot(h, wg_ref[...])
    act = (gate * _sigmoid(gate) * _dot(h, wu_ref[...])).astype(BF16)
    acc_scr[...] += _dot(act, wd_ref[...])

    @pl.when(kf == pl.num_programs(1) - 1)
    def _():
        o_ref[...] = _layer_norm(ALPHA * x_ref[...] + g_ref[...] * acc_scr[...], lg_ref[...], lb_ref[...])


def _ffn(x1, mod4, seq0, n_seq_tokens, w_gate, w_up, w_down, ln_g, ln_b):
    t, tf = x1.shape[0], FFN_TF
    tm = min(FFN_TM, n_seq_tokens)
    per_seq = n_seq_tokens // tm
    seq_of = lambda i, kf: seq0 + i // per_seq
    const = lambda i, kf: (0, 0)
    return pl.pallas_call(
        _ffn_kernel,
        grid=(t // tm, D_FF // tf),
        in_specs=[pl.BlockSpec((tm, D_MODEL), lambda i, kf: (i, 0)),
                  _mod_spec(3, seq_of), _mod_spec(4, seq_of), _mod_spec(5, seq_of),
                  pl.BlockSpec((D_MODEL, tf), lambda i, kf: (0, kf)),
                  pl.BlockSpec((D_MODEL, tf), lambda i, kf: (0, kf)),
                  pl.BlockSpec((tf, D_MODEL), lambda i, kf: (kf, 0)),
                  pl.BlockSpec((1, D_MODEL), const), pl.BlockSpec((1, D_MODEL), const)],
        out_specs=pl.BlockSpec((tm, D_MODEL), lambda i, kf: (i, 0)),
        out_shape=jax.ShapeDtypeStruct((t, D_MODEL), F32),
        scratch_shapes=[pltpu.VMEM((tm, D_MODEL), BF16), pltpu.VMEM((tm, D_MODEL), F32)],
        compiler_params=_cp(("arbitrary", "arbitrary")),
        name="ffn",
    )(x1, mod4, mod4, mod4, w_gate, w_up, w_down, ln_g.reshape(1, D_MODEL), ln_b.reshape(1, D_MODEL))


def _rope_tables(n_tokens):
    rows = n_tokens // GRID_W
    row = jnp.repeat(jnp.arange(rows), GRID_W).astype(F32)
    col = jnp.tile(jnp.arange(GRID_W), rows).astype(F32)
    freqs = ROPE_THETA ** (-jnp.arange(ROPE_AXIS_FREQS, dtype=F32) / ROPE_AXIS_FREQS)
    ang = jnp.concatenate([row[:, None] * freqs, col[:, None] * freqs], -1)
    cos, sin = jnp.cos(ang), jnp.sin(ang)
    cos2 = jnp.concatenate([cos, cos], -1)
    sin2 = jnp.concatenate([-sin, sin], -1)
    pad_lo = jnp.zeros((n_tokens, QK_NOPE), F32)
    pad_hi = jnp.zeros((n_tokens, QK_PAD - QK_NOPE - QK_ROPE), F32)
    q_cos = jnp.concatenate([pad_lo + 1.0, cos2, pad_hi], -1)
    q_sin = jnp.concatenate([pad_lo, sin2, pad_hi], -1)
    k_tab = jnp.concatenate([cos2, sin2], -1)
    return q_cos, q_sin, k_tab


def _swap_halves(w):
    half = w.shape[-1] // 2
    return jnp.concatenate([w[..., half:], w[..., :half]], -1)


def _block_diag_state(s):
    bsz = s.shape[0]
    s = s.reshape(bsz, HEAD_PAIRS, 2, RWKV_HEAD, RWKV_HEAD)
    z = jnp.zeros_like(s[:, :, 0])
    top = jnp.concatenate([s[:, :, 0], z], -1)
    bot = jnp.concatenate([z, s[:, :, 1]], -1)
    return jnp.concatenate([top, bot], -2)


def _unblock_diag_state(s_bd):
    bsz = s_bd.shape[0]
    s = jnp.stack([s_bd[:, :, :RWKV_HEAD, :RWKV_HEAD], s_bd[:, :, RWKV_HEAD:, RWKV_HEAD:]], 2)
    return s.reshape(bsz, RWKV_HEADS, RWKV_HEAD, RWKV_HEAD)


def _layer(x2, batch, n_tok, mod4, seq0, per_seq_mod, wts, rope_tabs, cache, states, is_context):
    mod_tokens = n_tok if per_seq_mod else 0
    proj_mla, proj_rw = _in_proj(x2, mod4, seq0, mod_tokens, wts["w_mla"], wts["w_rw"])
    q = _q_proj(proj_mla, n_tok, wts["q_norm_g"], wts["wq_full"], wts["wq_sw"],
                None if rope_tabs is None else rope_tabs[:2])
    kv = _kv_proj(proj_mla, OFF_KV // KV_LORA, proj_mla, OFF_KR // LANES, LANES, n_tok, wts["kv_norm_g"],
                  None if rope_tabs is None else rope_tabs[2], wts["wk_full"], wts["w_uv"], wts["e_mat"], is_context)
    kc = vc = None
    if cache is not None:
        kc, vc = _kv_proj(cache[0], 0, cache[1], 0, QK_ROPE, cache[0].shape[0] // batch, None, None,
                          wts["wk_full"], wts["w_uv"], wts["e_mat"], False)
    att = _attention(q, kv[0], kv[1], kc, vc, batch, n_tok)
    s_f = None if states is None else _block_diag_state(states[0])
    s_b = None if states is None else _block_diag_state(states[1])
    y_f, new_f = _scan(proj_rw, batch, n_tok, wts["rwkv"], 0, s_f, is_context, None)
    rw, new_b = _scan(proj_rw, batch, n_tok, wts["rwkv"], 1, s_b, is_context, y_f)
    seq_tokens = n_tok if per_seq_mod else x2.shape[0]
    x1 = _out_proj(att, rw, x2, mod4, seq0, seq_tokens, wts["w_out_att"], wts["w_out_rw"], wts["ln1_g"], wts["ln1_b"])
    y = _ffn(x1, mod4, seq0, seq_tokens, wts["w_ffn_gate"], wts["w_ffn_up"], wts["w_ffn_down"],
             wts["ln2_g"], wts["ln2_b"])
    return y, kv, new_f, new_b


def kernel(x_prompt, x_sample, cache_ckv, cache_krope, state_wkv_fwd, state_wkv_bwd, c, c_ctx, w_mod, b_mod, w_in, q_norm_g, kv_norm_g, w_uq, w_uk, w_uv, tok_shift_mu, w0_fwd, w_up_fwd, a0_fwd, a_up_fwd, w0_bwd, w_up_bwd, a0_bwd, a_up_bwd, g_up, k_k, k_a, r_k, gn_g, gn_b, w_out, ln1_g, ln1_b, w_ffn_gate, w_ffn_up, w_ffn_down, ln2_g, ln2_b):
    batch, seq = x_prompt.shape[:2]
    dec_batch, dec_seq = x_sample.shape[:2]
    layer = 0

    mod_rows = 16
    cond = jnp.concatenate([c, c_ctx[None, :], jnp.zeros((mod_rows - dec_batch - 1, D_MODEL), F32)], 0)
    mod4 = _modulation(cond, w_mod[layer], b_mod[layer]).reshape(mod_rows, 6, 1, D_MODEL)

    wi = w_in[layer]
    kr_w = wi[:, OFF_KR:OFF_RW]
    uq = w_uq[layer].reshape(Q_LORA, MLA_HEADS, QK_NOPE + QK_ROPE)
    uq_pad = jnp.zeros((Q_LORA, MLA_HEADS, QK_PAD - QK_NOPE - QK_ROPE), F32)
    wq_full = jnp.concatenate([uq, uq_pad], -1).reshape(Q_LORA, MLA_HEADS * QK_PAD)
    wq_sw = jnp.concatenate([jnp.zeros((Q_LORA, MLA_HEADS, QK_NOPE), F32), _swap_halves(uq[..., QK_NOPE:]), uq_pad],
                            -1).reshape(Q_LORA, MLA_HEADS * QK_PAD)
    uk = w_uk[layer].reshape(KV_LORA, MLA_HEADS, QK_NOPE)
    wk_full = jnp.concatenate([uk, jnp.zeros((KV_LORA, MLA_HEADS, QK_PAD - QK_NOPE), F32)], -1)
    e_head = jnp.concatenate([jnp.zeros((QK_ROPE, QK_NOPE), F32), jnp.eye(QK_ROPE, dtype=F32),
                              jnp.zeros((QK_ROPE, QK_PAD - QK_NOPE - QK_ROPE), F32)], -1)
    lora_pad = jnp.zeros((DECAY_LORA, RWKV_WIDTH), F32)
    row = lambda vec: vec[layer].reshape(1, -1)
    head_of = jnp.arange(2 * LANES) // RWKV_HEAD
    tri_lo = jnp.tril(jnp.ones((CHUNK, CHUNK), F32))
    rwkv = {
        "mu": row(tok_shift_mu), "k_k": row(k_k), "k_a": row(k_a), "r_k": row(r_k),
        "gn_g": row(gn_g), "gn_b": row(gn_b), "g_up": g_up[layer].astype(BF16),
        "dirs": [
            {"w0": row(w0_fwd), "a0": row(a0_fwd),
             "w_up": jnp.concatenate([w_up_fwd[layer], lora_pad], 0).astype(BF16),
             "a_up": jnp.concatenate([lora_pad, a_up_fwd[layer]], 0).astype(BF16)},
            {"w0": row(w0_bwd), "a0": row(a0_bwd),
             "w_up": jnp.concatenate([w_up_bwd[layer], lora_pad], 0).astype(BF16),
             "a_up": jnp.concatenate([lora_pad, a_up_bwd[layer]], 0).astype(BF16)}],
        "ones_bd": (head_of[:, None] == head_of[None, :]).astype(BF16),
        "tri": [tri_lo.astype(BF16), tri_lo.T.astype(BF16)],
    }
    wts = {
        "w_mla": jnp.concatenate([wi[:, :OFF_RW], _swap_halves(kr_w)], -1).astype(BF16),
        "w_rw": wi[:, OFF_RW:].astype(BF16),
        "q_norm_g": q_norm_g[layer], "kv_norm_g": kv_norm_g[layer],
        "wq_full": wq_full.astype(BF16), "wq_sw": wq_sw.astype(BF16),
        "wk_full": wk_full.reshape(KV_LORA, MLA_HEADS * QK_PAD).astype(BF16),
        "w_uv": w_uv[layer].astype(BF16),
        "e_mat": jnp.tile(e_head, (1, MLA_HEADS)).astype(BF16),
        "rwkv": rwkv,
        "w_out_att": w_out[layer][:MLA_HEADS * V_HEAD].astype(BF16),
        "w_out_rw": w_out[layer][MLA_HEADS * V_HEAD:].astype(BF16),
        "ln1_g": ln1_g[layer], "ln1_b": ln1_b[layer],
        "w_ffn_gate": w_ffn_gate[layer].astype(BF16), "w_ffn_up": w_ffn_up[layer].astype(BF16),
        "w_ffn_down": w_ffn_down[layer].astype(BF16),
        "ln2_g": ln2_g[layer], "ln2_b": ln2_b[layer],
    }

    y_prompt, kv_ctx, s_f, s_b = _layer(
        x_prompt.reshape(batch * seq, D_MODEL), batch, seq, mod4, dec_batch, False, wts, None, None, None, True)
    past = cache_ckv.shape[2]
    cache = (cache_ckv[:, layer].reshape(dec_batch * past, KV_LORA),
             cache_krope[:, layer].reshape(dec_batch * past, QK_ROPE))
    y_sample, _, _, _ = _layer(
        x_sample.reshape(dec_batch * dec_seq, D_MODEL), dec_batch, dec_seq, mod4, 0, True, wts,
        _rope_tables(dec_seq), cache, (state_wkv_fwd[:, layer], state_wkv_bwd[:, layer]), False)

    return (y_prompt.reshape(batch, seq, D_MODEL),
            y_sample.reshape(dec_batch, dec_seq, D_MODEL),
            kv_ctx[2].reshape(batch, 1, seq, KV_LORA),
            kv_ctx[3].reshape(batch, 1, seq, QK_ROPE),
            _unblock_diag_state(s_f)[:, None],
            _unblock_diag_state(s_b)[:, None])
```

```python
import functools
import math

import jax
import jax.numpy as jnp
from jax import lax
from jax.experimental import pallas as pl
from jax.experimental.pallas import tpu as pltpu

F32 = jnp.float32
BF16 = jnp.bfloat16

D_MODEL = 2048
GRID_W = 64
MLA_HEADS = 8
QK_NOPE = 128
QK_ROPE = 64
V_HEAD = 128
Q_LORA = 512
KV_LORA = 256
ROPE_AXIS_FREQS = QK_ROPE // 4
ROPE_THETA = 10000.0
RWKV_HEADS = 16
RWKV_HEAD = 64
RWKV_WIDTH = RWKV_HEADS * RWKV_HEAD
DECAY_LORA = 64
ICLR_LORA = 64
GATE_LORA = 128
OFF_KV = Q_LORA
OFF_KR = OFF_KV + KV_LORA
OFF_RW = OFF_KR + QK_ROPE
RW_COLS = 3 * RWKV_WIDTH + DECAY_LORA + ICLR_LORA + GATE_LORA
D_FF = 5632
LN_EPS = 1e-5
RMS_EPS = 1e-6
GN_EPS = 64e-5
DEPTH = 1
ALPHA = (2.0 * DEPTH) ** 0.25

LANES = 128
QK_PAD = 2 * LANES
MLA_COLS = OFF_RW + QK_ROPE
CHUNK = 64
HEAD_PAIRS = RWKV_HEADS // 2
ATTN_TQ = 512
ATTN_HEADS = 4
FFN_TM = 512
FFN_TF = 512
SCAN_TM = 512
SCAN_GROUP = 2
SCAN_PAIRS = 8
VMEM_LIMIT = 56 * 1024 * 1024


def _cp(sem):
    return pltpu.CompilerParams(dimension_semantics=sem, vmem_limit_bytes=VMEM_LIMIT)


def _dot(a, b):
    return jnp.dot(a, b, preferred_element_type=F32)


def _dot_nt(a, b):
    return lax.dot_general(a, b, (((1,), (1,)), ((), ())), preferred_element_type=F32)


def _dot_tn(a, b):
    return lax.dot_general(a, b, (((0,), (0,)), ((), ())), preferred_element_type=F32)


def _sigmoid(x):
    return 1.0 / (1.0 + jnp.exp(-x))


def _layer_norm(h, g, b):
    mu = jnp.mean(h, -1, keepdims=True)
    d = h - mu
    var = jnp.mean(d * d, -1, keepdims=True)
    return d * lax.rsqrt(var + LN_EPS) * g + b


def _sigmoid_tanh(x):
    return 0.5 * jnp.tanh(0.5 * x) + 0.5


def _split2(x):
    hi = x.astype(BF16)
    return hi, (x - hi.astype(F32)).astype(BF16)


def _dot2_right(x, m):
    hi, lo = _split2(x)
    return _dot(hi, m) + _dot(lo, m)


def _dot2_left(m, x):
    hi, lo = _split2(x)
    return _dot(m, hi) + _dot(m, lo)


def _mod_kernel(c_ref, w_ref, b_ref, o_ref):
    c = c_ref[...]
    s = (c * _sigmoid(c)).astype(BF16)
    o_ref[...] = _dot(s, w_ref[...].astype(BF16)) + b_ref[...]


def _modulation(cond, w_mod, b_mod):
    rows, tn = cond.shape[0], 1024
    n = w_mod.shape[1]
    return pl.pallas_call(
        _mod_kernel,
        grid=(n // tn,),
        in_specs=[pl.BlockSpec((rows, D_MODEL), lambda j: (0, 0)),
                  pl.BlockSpec((D_MODEL, tn), lambda j: (0, j)),
                  pl.BlockSpec((1, tn), lambda j: (0, j))],
        out_specs=pl.BlockSpec((rows, tn), lambda j: (0, j)),
        out_shape=jax.ShapeDtypeStruct((rows, n), F32),
        compiler_params=_cp(("arbitrary",)),
        name="modulation",
    )(cond, w_mod, b_mod.reshape(1, n))


def _mod_spec(which, seq_of):
    return pl.BlockSpec((None, None, 1, D_MODEL), lambda *g: (seq_of(*g), which, 0, 0))


def _in_kernel(x_ref, sh_ref, sc_ref, wm_ref, wr_ref, mu_ref, om_ref, or_ref, oe_ref):
    xm = (x_ref[...] * (1.0 + sc_ref[...]) + sh_ref[...]).astype(BF16)
    om_ref[...] = _dot(xm, wm_ref[...])
    u = _dot(xm, wr_ref[...])
    tm = u.shape[0]
    row = lax.broadcasted_iota(jnp.int32, u.shape, 0)
    up = jnp.where(row == 0, 0.0, pltpu.roll(u, 1, 0))
    un = jnp.where(row == tm - 1, 0.0, pltpu.roll(u, tm - 1, 0))
    or_ref[...] = u + mu_ref[...] * (0.5 * (up + un) - u)
    oe_ref[...] = jnp.concatenate([u[0:1], u[tm - 1:tm], jnp.zeros((6, u.shape[1]), F32)], axis=0)


def _in_proj(x2, mod4, seq0, n_seq_tokens, tm, w_mla, w_rw, mu):
    t = x2.shape[0]
    per_seq = n_seq_tokens // tm
    seq_of = (lambda i: seq0) if per_seq == 0 else (lambda i: seq0 + i // per_seq)
    resident = pl.Buffered(1)
    return pl.pallas_call(
        _in_kernel,
        grid=(t // tm,),
        in_specs=[pl.BlockSpec((tm, D_MODEL), lambda i: (i, 0)),
                  _mod_spec(0, seq_of), _mod_spec(1, seq_of),
                  pl.BlockSpec((D_MODEL, MLA_COLS), lambda i: (0, 0), pipeline_mode=resident),
                  pl.BlockSpec((D_MODEL, RW_COLS), lambda i: (0, 0), pipeline_mode=resident),
                  pl.BlockSpec((1, RW_COLS), lambda i: (0, 0))],
        out_specs=[pl.BlockSpec((tm, MLA_COLS), lambda i: (i, 0)),
                   pl.BlockSpec((tm, RW_COLS), lambda i: (i, 0)),
                   pl.BlockSpec((8, RW_COLS), lambda i: (i, 0))],
        out_shape=[jax.ShapeDtypeStruct((t, MLA_COLS), F32),
                   jax.ShapeDtypeStruct((t, RW_COLS), F32),
                   jax.ShapeDtypeStruct((t // tm * 8, RW_COLS), F32)],
        compiler_params=_cp(("arbitrary",)),
        name="in_proj",
    )(x2, mod4, mod4, w_mla, w_rw, mu)


def _q_kernel(*refs, rope):
    if rope:
        qd_ref, g_ref, w_ref, wsw_ref, c_ref, s_ref, o_ref = refs
    else:
        qd_ref, g_ref, w_ref, o_ref = refs
    x = qd_ref[...]
    qn = (x * lax.rsqrt(jnp.mean(x * x, -1, keepdims=True) + RMS_EPS) * g_ref[...]).astype(BF16)
    q = _dot(qn, w_ref[...])
    if rope:
        c = jnp.concatenate([c_ref[...]] * MLA_HEADS, axis=1)
        s = jnp.concatenate([s_ref[...]] * MLA_HEADS, axis=1)
        q = q * c + _dot(qn, wsw_ref[...]) * s
    o_ref[...] = q.astype(BF16)


def _q_proj(proj_mla, n_seq_tokens, q_norm_g, wq_full, wq_sw, rope_tabs):
    t, tm = proj_mla.shape[0], 256
    per_seq = n_seq_tokens // tm
    rope = rope_tabs is not None
    width = MLA_HEADS * QK_PAD
    const = lambda i: (0, 0)
    in_specs = [pl.BlockSpec((tm, Q_LORA), lambda i: (i, 0)),
                pl.BlockSpec((1, Q_LORA), const),
                pl.BlockSpec((Q_LORA, width), const)]
    args = [proj_mla, q_norm_g.reshape(1, Q_LORA), wq_full]
    if rope:
        in_specs += [pl.BlockSpec((Q_LORA, width), const),
                     pl.BlockSpec((tm, QK_PAD), lambda i: (i % per_seq, 0)),
                     pl.BlockSpec((tm, QK_PAD), lambda i: (i % per_seq, 0))]
        args += [wq_sw, rope_tabs[0], rope_tabs[1]]
    return pl.pallas_call(
        functools.partial(_q_kernel, rope=rope),
        grid=(t // tm,),
        in_specs=in_specs,
        out_specs=pl.BlockSpec((tm, width), lambda i: (i, 0)),
        out_shape=jax.ShapeDtypeStruct((t, width), BF16),
        compiler_params=_cp(("arbitrary",)),
        name="q_proj",
    )(*args)


def _kv_kernel(*refs, norm, rope, emit_new):
    it = iter(refs)
    ckv_ref, kr_ref = next(it), next(it)
    g_ref = next(it) if norm else None
    cs_ref = next(it) if rope else None
    wk_ref, wv_ref, e_ref = next(it), next(it), next(it)
    k_out, v_out = next(it), next(it)
    x = ckv_ref[...]
    if norm:
        x = x * lax.rsqrt(jnp.mean(x * x, -1, keepdims=True) + RMS_EPS) * g_ref[...]
    xb = x.astype(BF16)
    kr = kr_ref[...]
    if rope:
        y = kr * cs_ref[...]
        kr = y + pltpu.roll(y, QK_ROPE, 1)
    kr = kr[:, :QK_ROPE]
    k_out[...] = (_dot(xb, wk_ref[...]) + _dot(kr.astype(BF16), e_ref[...])).T.astype(BF16)
    v_out[...] = _dot(xb, wv_ref[...]).astype(BF16)
    if emit_new:
        nc_out, nk_out = next(it), next(it)
        nc_out[...] = x
        nk_out[...] = kr


def _kv_proj(ckv_src, ckv_blk, kr_src, kr_blk, kr_w, n_seq_tokens, kv_norm_g, cs_tab, wk_full, wv, e_mat, emit_new):
    t, tm = ckv_src.shape[0], 256
    per_seq = n_seq_tokens // tm
    norm, rope = kv_norm_g is not None, cs_tab is not None
    kw, vw = MLA_HEADS * QK_PAD, MLA_HEADS * V_HEAD
    const = lambda i: (0, 0)
    in_specs = [pl.BlockSpec((tm, KV_LORA), lambda i: (i, ckv_blk)),
                pl.BlockSpec((tm, kr_w), lambda i: (i, kr_blk))]
    args = [ckv_src, kr_src]
    if norm:
        in_specs.append(pl.BlockSpec((1, KV_LORA), const))
        args.append(kv_norm_g.reshape(1, KV_LORA))
    if rope:
        in_specs.append(pl.BlockSpec((tm, LANES), lambda i: (i % per_seq, 0)))
        args.append(cs_tab)
    in_specs += [pl.BlockSpec((KV_LORA, kw), const), pl.BlockSpec((KV_LORA, vw), const),
                 pl.BlockSpec((QK_ROPE, kw), const)]
    args += [wk_full, wv, e_mat]
    out_specs = [pl.BlockSpec((kw, tm), lambda i: (i // per_seq, i % per_seq)),
                 pl.BlockSpec((tm, vw), lambda i: (i, 0))]
    out_shape = [jax.ShapeDtypeStruct((t // n_seq_tokens * kw, n_seq_tokens), BF16),
                 jax.ShapeDtypeStruct((t, vw), BF16)]
    if emit_new:
        out_specs += [pl.BlockSpec((tm, KV_LORA), lambda i: (i, 0)), pl.BlockSpec((tm, QK_ROPE), lambda i: (i, 0))]
        out_shape += [jax.ShapeDtypeStruct((t, KV_LORA), F32), jax.ShapeDtypeStruct((t, QK_ROPE), F32)]
    return pl.pallas_call(
        functools.partial(_kv_kernel, norm=norm, rope=rope, emit_new=emit_new),
        grid=(t // tm,),
        in_specs=in_specs, out_specs=out_specs, out_shape=out_shape,
        compiler_params=_cp(("arbitrary",)),
        name="kv_proj",
    )(*args)


def _attn_kernel(*refs, cache, heads, exp2_scale):
    if cache:
        q_ref, k_ref, v_ref, kc_ref, vc_ref, o_ref = refs
    else:
        q_ref, k_ref, v_ref, o_ref = refs
    out = [None] * heads

    def head_stages(h):
        qk = slice(h * QK_PAD, (h + 1) * QK_PAD)
        hv = slice(h * V_HEAD, (h + 1) * V_HEAD)
        q = q_ref[:, qk]
        s = _dot(q, k_ref[qk, :])
        sc = _dot(q, kc_ref[qk, :]) if cache else None
        yield
        m = jnp.max(s, -1, keepdims=True)
        if cache:
            m = jnp.maximum(m, jnp.max(sc, -1, keepdims=True))
        yield
        p = jnp.exp2((s - m) * exp2_scale)
        l = jnp.sum(p, -1, keepdims=True)
        if cache:
            pc = jnp.exp2((sc - m) * exp2_scale)
            l = l + jnp.sum(pc, -1, keepdims=True)
        yield
        acc = _dot(p.astype(BF16), v_ref[:, hv])
        if cache:
            acc = acc + _dot(pc.astype(BF16), vc_ref[:, hv])
        out[h] = acc * (1.0 / l)
        yield

    live, waiting = [], [head_stages(h) for h in range(heads)]
    while live or waiting:
        if waiting:
            live.append(waiting.pop(0))
        live = [g for g in live if next(g, StopIteration) is not StopIteration]
    o_ref[...] = jnp.concatenate(out, axis=1).astype(BF16)


def _attention(q, kt, v, kct, vc, batch, n_tok):
    tq, heads = min(ATTN_TQ, n_tok), ATTN_HEADS
    n_q = n_tok // tq
    groups = MLA_HEADS // heads
    cache = kct is not None
    in_specs = [pl.BlockSpec((tq, heads * QK_PAD), lambda b, h, i: (b * n_q + i, h)),
                pl.BlockSpec((heads * QK_PAD, n_tok), lambda b, h, i: (b * groups + h, 0)),
                pl.BlockSpec((n_tok, heads * V_HEAD), lambda b, h, i: (b, h))]
    args = [q, kt, v]
    if cache:
        n_c = vc.shape[0] // batch
        in_specs += [pl.BlockSpec((heads * QK_PAD, n_c), lambda b, h, i: (b * groups + h, 0)),
                     pl.BlockSpec((n_c, heads * V_HEAD), lambda b, h, i: (b, h))]
        args += [kct, vc]
    scale = 1.0 / math.sqrt(QK_NOPE + QK_ROPE)
    return pl.pallas_call(
        functools.partial(_attn_kernel, cache=cache, heads=heads, exp2_scale=scale * math.log2(math.e)),
        grid=(batch, groups, n_q),
        in_specs=in_specs,
        out_specs=pl.BlockSpec((tq, heads * V_HEAD), lambda b, h, i: (b * n_q + i, h)),
        out_shape=jax.ShapeDtypeStruct((batch * n_tok, MLA_HEADS * V_HEAD), BF16),
        compiler_params=_cp(("arbitrary", "arbitrary", "arbitrary")),
        name="attention",
    )(*args)


def _pair_blockdiag(y, lane_lo):
    zero = jnp.zeros_like(y)
    return jnp.concatenate([jnp.where(lane_lo, y, zero), jnp.where(lane_lo, zero, y)], axis=0)


def _scan_kernel(*refs, tm, npair, reverse, zero_init, emit_state, finalize):
    it = iter(refs)
    blocks = {}
    cols = ("r", "k", "v", "wa") + (("gd",) if finalize else ())
    for name in cols:
        blocks[name] = (next(it), next(it), next(it), next(it))
    kk_ref, ka_ref, w0_ref, wup_ref, a0_ref, aup_ref = (next(it) for _ in range(6))
    if finalize:
        rk_ref, a0o_ref, aupo_ref, gup_ref, gng_ref, gnb_ref, yo_ref = (next(it) for _ in range(7))
    ones_ref, tri_ref = next(it), next(it)
    s0_ref = None if zero_init else next(it)
    y_ref = next(it)
    sout_ref = next(it) if emit_state else None
    s_scr = next(it)

    t_idx = pl.program_id(2)
    n_t = pl.num_programs(2)
    tile = (n_t - 1 - t_idx) if reverse else t_idx
    first_tok = tile == 0
    last_tok = tile == n_t - 1

    @pl.when(t_idx == 0)
    def _():
        if zero_init:
            s_scr[...] = jnp.zeros_like(s_scr)
        else:
            s_scr[...] = s0_ref[...]

    def shifted(name):
        main, prv, nxt, mu = blocks[name]
        half_mu = 0.5 * mu[...]
        row8 = lax.broadcasted_iota(jnp.int32, (8, main.shape[1]), 0)
        p_row = jnp.where(first_tok, 0.0, prv[1:2, :])
        n_row = jnp.where(last_tok, 0.0, nxt[0:1, :])
        head = main[0:8, :] + half_mu * jnp.where(row8 == 0, p_row, 0.0)
        tail = main[tm - 8:tm, :] + half_mu * jnp.where(row8 == 7, n_row, 0.0)
        return jnp.concatenate([head, main[8:tm - 8, :], tail], axis=0)

    ones_bd = ones_ref[...]

    seg_w = ones_bd.shape[0]

    def seg(x):
        return jnp.concatenate(
            [_dot2_right(x[:, j * seg_w:(j + 1) * seg_w], ones_bd) for j in range(x.shape[1] // seg_w)], axis=1)

    r, k, v, wa = shifted("r"), shifted("k"), shifted("v"), shifted("wa")
    wab = wa.astype(BF16)
    kk = k * kk_ref[...]
    kk = kk * lax.rsqrt(jnp.maximum(seg(kk * kk), 1e-24))
    ld = _sigmoid_tanh(w0_ref[...] + _dot(jnp.tanh(wa).astype(BF16), wup_ref[...])) * (-math.exp(-0.5))
    a = _sigmoid_tanh(a0_ref[...] + _dot(wab, aup_ref[...]))
    kd = k * (1.0 + (a - 1.0) * ka_ref[...])
    b = kk * a

    c_row = lax.broadcasted_iota(jnp.int32, (CHUNK, LANES), 0)
    c_lane = lax.broadcasted_iota(jnp.int32, (CHUNK, LANES), 1)
    c_src = c_lane & (CHUNK - 1)
    lane_lo = c_lane < CHUNK
    strict = (c_src > c_row) if reverse else (c_src < c_row)
    incl = (c_src >= c_row) if reverse else (c_src <= c_row)
    eye = (c_src == c_row).astype(F32)
    bd_row = lax.broadcasted_iota(jnp.int32, (LANES, LANES), 0)
    bd_lane = lax.broadcasted_iota(jnp.int32, (LANES, LANES), 1)
    bd_mask = jnp.right_shift(bd_row, 6) == jnp.right_shift(bd_lane, 6)
    tri = tri_ref[...]

    def nn(p, y):
        return _dot(p.astype(BF16), _pair_blockdiag(y.astype(BF16), lane_lo))

    n_chunks = tm // CHUNK
    order = list(range(n_chunks - 1, -1, -1) if reverse else range(n_chunks))
    cums = [_dot2_left(tri, ld[c * CHUNK:(c + 1) * CHUNK]) for c in range(n_chunks)]
    pairs = range(npair)
    strict2 = jnp.concatenate([strict, strict], axis=1)
    incl2 = jnp.concatenate([incl, incl], axis=1)
    st = {}

    def independent_stage(c):
        sl = slice(c * CHUNK, (c + 1) * CHUNK)
        for j in pairs:
            ln = slice(j * LANES, (j + 1) * LANES)
            r_c, v_c, kk_c, kd_c, b_c, ld_c = r[sl, ln], v[sl, ln], kk[sl, ln], kd[sl, ln], b[sl, ln], ld[sl, ln]
            cum = cums[c][:, ln]
            tot = cum[0:1] if reverse else cum[CHUNK - 1:CHUNK]
            e_neg = jnp.exp(-cum)
            e_tot = jnp.exp(tot - cum)
            a_hat = -kk_c * jnp.exp(cum - ld_c)
            r_hat = r_c * jnp.exp(cum)
            ar = jnp.concatenate([a_hat, r_hat], axis=0).astype(BF16)
            bk_hat = jnp.concatenate([_pair_blockdiag((b_c * e_neg).astype(BF16), lane_lo),
                                      _pair_blockdiag((kd_c * e_neg).astype(BF16), lane_lo)], axis=0)
            g = _dot_nt(ar, bk_hat)
            gs = jnp.where(strict2, g[:CHUNK], 0.0)
            lmat = gs[:, :LANES]
            st[j, c] = dict(
                ar=ar, v=v_c.astype(BF16), v_bd=_pair_blockdiag(v_c.astype(BF16), lane_lo),
                p=eye + lmat, q=lmat, akm=gs[:, LANES:],
                rbk=jnp.where(incl2, g[CHUNK:], 0.0).astype(BF16),
                bk=jnp.concatenate([b_c * e_tot, kd_c * e_tot], axis=0).astype(BF16),
                w=jnp.exp(tot))
        yield
        for j in pairs:
            st[j, c]["q"] = nn(st[j, c]["q"], st[j, c]["q"])
        yield
        for _ in range(int(math.log2(CHUNK)) - 2):
            for j in pairs:
                d = st[j, c]
                pq = nn(jnp.concatenate([d["p"], d["q"]], axis=0), d["q"])
                d["p"], d["q"] = d["p"] + pq[:CHUNK], pq[CHUNK:]
            yield
        for j in pairs:
            d = st[j, c]
            d["p"] = d["p"] + nn(d["p"], d["q"])
            d["akv"] = _dot(d["akm"].astype(BF16), d["v_bd"])
        yield

    s_cur = [s_scr[j] for j in pairs]
    y_cells = {}

    def dependent_stage(c):
        xs = [_dot_nt(st[j, c]["ar"], s_cur[j].astype(BF16)) for j in pairs]
        yield
        u = [nn(st[j, c]["p"], xs[j][:CHUNK] + st[j, c]["akv"]) for j in pairs]
        yield
        for j in pairs:
            d = st[j, c]
            y_cells[j, c] = xs[j][CHUNK:] + _dot(
                d["rbk"],
                jnp.concatenate([_pair_blockdiag(u[j].astype(BF16), lane_lo), d["v_bd"]], axis=0))
            upd = _dot_tn(jnp.concatenate([u[j].astype(BF16), d["v"]], axis=0), d["bk"])
            s_cur[j] = s_cur[j] * d["w"] + jnp.where(bd_mask, upd, 0.0)
        yield

    def run_side_by_side(*stages):
        live = list(stages)
        while live:
            for g in list(live):
                if next(g, StopIteration) is StopIteration:
                    live.remove(g)

    def in_sequence(stages):
        for g in stages:
            yield from g

    groups = [order[i:i + SCAN_GROUP] for i in range(0, n_chunks, SCAN_GROUP)]
    for gi in range(len(groups) + 1):
        stages = [independent_stage(c) for c in groups[gi]] if gi < len(groups) else []
        if gi > 0:
            stages.append(in_sequence([dependent_stage(c) for c in groups[gi - 1]]))
        run_side_by_side(*stages)
    s_new = s_cur
    for j in range(npair):
        s_scr[j] = s_new[j]
    y = jnp.concatenate(
        [jnp.concatenate([y_cells[j, c] for c in range(n_chunks)], axis=0) for j in range(npair)], axis=1)

    if emit_state:
        @pl.when(t_idx == n_t - 1)
        def _():
            for j in range(npair):
                sout_ref[j] = s_new[j]

    if not finalize:
        y_ref[...] = y
    else:
        rk = rk_ref[...]
        a_o = _sigmoid_tanh(a0o_ref[...] + _dot(wab, aupo_ref[...]))
        kd_o = k * (1.0 + (a_o - 1.0) * ka_ref[...])
        bonus = seg(r * rk * (kd + kd_o))
        gate = _dot(_sigmoid_tanh(shifted("gd")).astype(BF16), gup_ref[...])
        yt = y + yo_ref[...]
        inv_n = 1.0 / RWKV_HEAD
        d = yt - seg(yt) * inv_n
        var = seg(d * d) * inv_n
        yn = d * lax.rsqrt(var + GN_EPS) * gng_ref[...] + gnb_ref[...]
        y_ref[...] = ((yn + bonus * v) * gate).astype(BF16)


def _scan(proj_rw, edge_rows, batch, n_tok, prm, direction, s0_bd, emit_state, y_other):
    reverse = direction == 1
    finalize = y_other is not None
    zero_init = s0_bd is None
    t = batch * n_tok
    tm = min(SCAN_TM, n_tok)
    npair = SCAN_PAIRS
    width = npair * LANES
    groups = HEAD_PAIRS // npair
    n_t = n_tok // tm

    def tile_of(ti):
        return (n_t - 1 - ti) if reverse else ti

    def token_specs(w, cb):
        return [pl.BlockSpec((tm, w), lambda bb, p, ti: (bb * n_t + tile_of(ti), cb(p))),
                pl.BlockSpec((8, w), lambda bb, p, ti: (jnp.maximum(bb * n_t + tile_of(ti) - 1, 0), cb(p))),
                pl.BlockSpec((8, w), lambda bb, p, ti: (jnp.minimum(bb * n_t + tile_of(ti) + 1, batch * n_t - 1),
                                                        cb(p))),
                pl.BlockSpec((1, w), lambda bb, p, ti: (0, cb(p)))]

    rowvec = lambda: pl.BlockSpec((1, width), lambda bb, p, ti: (0, p))
    lora = lambda: pl.BlockSpec((LANES, width), lambda bb, p, ti: (0, p))
    const = lambda shape: pl.BlockSpec(shape, lambda bb, p, ti: (0, 0))
    state = lambda: pl.BlockSpec((None, npair, LANES, LANES), lambda bb, p, ti: (bb, p, 0, 0))
    y_spec = lambda: pl.BlockSpec((tm, width), lambda bb, p, ti: (bb * n_t + tile_of(ti), p))

    lora_blk = 3 * RWKV_WIDTH // LANES
    in_specs, args = [], []
    for name, w, cb in (("r", width, lambda p: p), ("k", width, lambda p: groups + p),
                        ("v", width, lambda p: 2 * groups + p), ("wa", LANES, lambda p: lora_blk),
                        ("gd", LANES, lambda p: lora_blk + 1)):
        if name == "gd" and not finalize:
            continue
        in_specs += token_specs(w, cb)
        args += [proj_rw, edge_rows, edge_rows, prm["mu"]]
    d, o = prm["dirs"][direction], prm["dirs"][1 - direction]
    in_specs += [rowvec(), rowvec(), rowvec(), lora(), rowvec(), lora()]
    args += [prm["k_k"], prm["k_a"], d["w0"], d["w_up"], d["a0"], d["a_up"]]
    if finalize:
        in_specs += [rowvec(), rowvec(), lora(), lora(), rowvec(), rowvec(), y_spec()]
        args += [prm["r_k"], o["a0"], o["a_up"], prm["g_up"], prm["gn_g"], prm["gn_b"], y_other]
    in_specs += [const(prm["ones_bd"].shape), const((CHUNK, CHUNK))]
    args += [prm["ones_bd"], prm["tri"][direction]]
    if not zero_init:
        in_specs.append(state())
        args.append(s0_bd)
    out_specs = [y_spec()]
    out_shape = [jax.ShapeDtypeStruct((t, RWKV_WIDTH), BF16 if finalize else F32)]
    if emit_state:
        out_specs.append(state())
        out_shape.append(jax.ShapeDtypeStruct((batch, HEAD_PAIRS, LANES, LANES), F32))
    out = pl.pallas_call(
        functools.partial(_scan_kernel, tm=tm, npair=npair, reverse=reverse, zero_init=zero_init,
                          emit_state=emit_state, finalize=finalize),
        grid=(batch, groups, n_t),
        in_specs=in_specs, out_specs=out_specs, out_shape=out_shape,
        scratch_shapes=[pltpu.VMEM((npair, LANES, LANES), F32)],
        compiler_params=_cp(("arbitrary", "arbitrary", "arbitrary")),
        name="rwkv_scan_bwd" if reverse else "rwkv_scan_fwd",
    )(*args)
    return out if emit_state else (out[0], None)


def _out_kernel(att_ref, rw_ref, x_ref, g_ref, wa_ref, wr_ref, lg_ref, lb_ref, o_ref):
    f = _dot(att_ref[...], wa_ref[...]) + _dot(rw_ref[...], wr_ref[...])
    o_ref[...] = _layer_norm(ALPHA * x_ref[...] + g_ref[...] * f, lg_ref[...], lb_ref[...])


def _out_proj(att, rw, x2, mod4, seq0, n_seq_tokens, w_att, w_rw, ln_g, ln_b):
    t, tm = x2.shape[0], 256
    per_seq = n_seq_tokens // tm
    seq_of = lambda i: seq0 + i // per_seq
    half = D_MODEL // 2
    const = lambda i: (0, 0)
    return pl.pallas_call(
        _out_kernel,
        grid=(t // tm,),
        in_specs=[pl.BlockSpec((tm, half), lambda i: (i, 0)), pl.BlockSpec((tm, half), lambda i: (i, 0)),
                  pl.BlockSpec((tm, D_MODEL), lambda i: (i, 0)), _mod_spec(2, seq_of),
                  pl.BlockSpec((half, D_MODEL), const), pl.BlockSpec((half, D_MODEL), const),
                  pl.BlockSpec((1, D_MODEL), const), pl.BlockSpec((1, D_MODEL), const)],
        out_specs=pl.BlockSpec((tm, D_MODEL), lambda i: (i, 0)),
        out_shape=jax.ShapeDtypeStruct((t, D_MODEL), F32),
        compiler_params=_cp(("arbitrary",)),
        name="out_proj",
    )(att, rw, x2, mod4, w_att, w_rw, ln_g.reshape(1, D_MODEL), ln_b.reshape(1, D_MODEL))


def _ffn_kernel(x_ref, sh_ref, sc_ref, g_ref, wg_ref, wu_ref, wd_ref, lg_ref, lb_ref, o_ref, h_scr, acc_scr):
    kf = pl.program_id(1)

    @pl.when(kf == 0)
    def _():
        h_scr[...] = (x_ref[...] * (1.0 + sc_ref[...]) + sh_ref[...]).astype(BF16)
        acc_scr[...] = jnp.zeros_like(acc_scr)

    h = h_scr[...]
    gate = _dot(h, wg_ref[...])
    act = (gate * _sigmoid(gate) * _dot(h, wu_ref[...])).astype(BF16)
    acc_scr[...] += _dot(act, wd_ref[...])

    @pl.when(kf == pl.num_programs(1) - 1)
    def _():
        o_ref[...] = _layer_norm(ALPHA * x_ref[...] + g_ref[...] * acc_scr[...], lg_ref[...], lb_ref[...])


def _ffn(x1, mod4, seq0, n_seq_tokens, w_gate, w_up, w_down, ln_g, ln_b):
    t, tf = x1.shape[0], FFN_TF
    tm = min(FFN_TM, n_seq_tokens)
    per_seq = n_seq_tokens // tm
    seq_of = lambda i, kf: seq0 + i // per_seq
    const = lambda i, kf: (0, 0)
    return pl.pallas_call(
        _ffn_kernel,
        grid=(t // tm, D_FF // tf),
        in_specs=[pl.BlockSpec((tm, D_MODEL), lambda i, kf: (i, 0)),
                  _mod_spec(3, seq_of), _mod_spec(4, seq_of), _mod_spec(5, seq_of),
                  pl.BlockSpec((D_MODEL, tf), lambda i, kf: (0, kf)),
                  pl.BlockSpec((D_MODEL, tf), lambda i, kf: (0, kf)),
                  pl.BlockSpec((tf, D_MODEL), lambda i, kf: (kf, 0)),
                  pl.BlockSpec((1, D_MODEL), const), pl.BlockSpec((1, D_MODEL), const)],
        out_specs=pl.BlockSpec((tm, D_MODEL), lambda i, kf: (i, 0)),
        out_shape=jax.ShapeDtypeStruct((t, D_MODEL), F32),
        scratch_shapes=[pltpu.VMEM((tm, D_MODEL), BF16), pltpu.VMEM((tm, D_MODEL), F32)],
        compiler_params=_cp(("arbitrary", "arbitrary")),
        name="ffn",
    )(x1, mod4, mod4, mod4, w_gate, w_up, w_down, ln_g.reshape(1, D_MODEL), ln_b.reshape(1, D_MODEL))


def _rope_tables(n_tokens):
    rows = n_tokens // GRID_W
    row = jnp.repeat(jnp.arange(rows), GRID_W).astype(F32)
    col = jnp.tile(jnp.arange(GRID_W), rows).astype(F32)
    freqs = ROPE_THETA ** (-jnp.arange(ROPE_AXIS_FREQS, dtype=F32) / ROPE_AXIS_FREQS)
    ang = jnp.concatenate([row[:, None] * freqs, col[:, None] * freqs], -1)
    cos, sin = jnp.cos(ang), jnp.sin(ang)
    cos2 = jnp.concatenate([cos, cos], -1)
    sin2 = jnp.concatenate([-sin, sin], -1)
    pad_lo = jnp.zeros((n_tokens, QK_NOPE), F32)
    pad_hi = jnp.zeros((n_tokens, QK_PAD - QK_NOPE - QK_ROPE), F32)
    q_cos = jnp.concatenate([pad_lo + 1.0, cos2, pad_hi], -1)
    q_sin = jnp.concatenate([pad_lo, sin2, pad_hi], -1)
    k_tab = jnp.concatenate([cos2, sin2], -1)
    return q_cos, q_sin, k_tab


def _swap_halves(w):
    half = w.shape[-1] // 2
    return jnp.concatenate([w[..., half:], w[..., :half]], -1)


def _block_diag_state(s):
    bsz = s.shape[0]
    s = s.reshape(bsz, HEAD_PAIRS, 2, RWKV_HEAD, RWKV_HEAD)
    z = jnp.zeros_like(s[:, :, 0])
    top = jnp.concatenate([s[:, :, 0], z], -1)
    bot = jnp.concatenate([z, s[:, :, 1]], -1)
    return jnp.concatenate([top, bot], -2)


def _unblock_diag_state(s_bd):
    bsz = s_bd.shape[0]
    s = jnp.stack([s_bd[:, :, :RWKV_HEAD, :RWKV_HEAD], s_bd[:, :, RWKV_HEAD:, RWKV_HEAD:]], 2)
    return s.reshape(bsz, RWKV_HEADS, RWKV_HEAD, RWKV_HEAD)


def _layer(x2, batch, n_tok, mod4, seq0, per_seq_mod, wts, rope_tabs, cache, states, is_context):
    mod_tokens = n_tok if per_seq_mod else 0
    proj_mla, proj_rw, edge_rows = _in_proj(x2, mod4, seq0, mod_tokens, min(SCAN_TM, n_tok), wts["w_mla"],
                                            wts["w_rw"], wts["rwkv"]["mu"])
    q = _q_proj(proj_mla, n_tok, wts["q_norm_g"], wts["wq_full"], wts["wq_sw"],
                None if rope_tabs is None else rope_tabs[:2])
    kv = _kv_proj(proj_mla, OFF_KV // KV_LORA, proj_mla, OFF_KR // LANES, LANES, n_tok, wts["kv_norm_g"],
                  None if rope_tabs is None else rope_tabs[2], wts["wk_full"], wts["w_uv"], wts["e_mat"], is_context)
    kc = vc = None
    if cache is not None:
        kc, vc = _kv_proj(cache[0], 0, cache[1], 0, QK_ROPE, cache[0].shape[0] // batch, None, None,
                          wts["wk_full"], wts["w_uv"], wts["e_mat"], False)
    att = _attention(q, kv[0], kv[1], kc, vc, batch, n_tok)
    s_f = None if states is None else _block_diag_state(states[0])
    s_b = None if states is None else _block_diag_state(states[1])
    y_f, new_f = _scan(proj_rw, edge_rows, batch, n_tok, wts["rwkv"], 0, s_f, is_context, None)
    rw, new_b = _scan(proj_rw, edge_rows, batch, n_tok, wts["rwkv"], 1, s_b, is_context, y_f)
    seq_tokens = n_tok if per_seq_mod else x2.shape[0]
    x1 = _out_proj(att, rw, x2, mod4, seq0, seq_tokens, wts["w_out_att"], wts["w_out_rw"], wts["ln1_g"], wts["ln1_b"])
    y = _ffn(x1, mod4, seq0, seq_tokens, wts["w_ffn_gate"], wts["w_ffn_up"], wts["w_ffn_down"],
             wts["ln2_g"], wts["ln2_b"])
    return y, kv, new_f, new_b


def kernel(x_prompt, x_sample, cache_ckv, cache_krope, state_wkv_fwd, state_wkv_bwd, c, c_ctx, w_mod, b_mod, w_in, q_norm_g, kv_norm_g, w_uq, w_uk, w_uv, tok_shift_mu, w0_fwd, w_up_fwd, a0_fwd, a_up_fwd, w0_bwd, w_up_bwd, a0_bwd, a_up_bwd, g_up, k_k, k_a, r_k, gn_g, gn_b, w_out, ln1_g, ln1_b, w_ffn_gate, w_ffn_up, w_ffn_down, ln2_g, ln2_b):
    batch, seq = x_prompt.shape[:2]
    dec_batch, dec_seq = x_sample.shape[:2]
    layer = 0

    mod_rows = 16
    cond = jnp.concatenate([c, c_ctx[None, :], jnp.zeros((mod_rows - dec_batch - 1, D_MODEL), F32)], 0)
    mod4 = _modulation(cond, w_mod[layer], b_mod[layer]).reshape(mod_rows, 6, 1, D_MODEL)

    wi = w_in[layer].astype(BF16)
    kr_w = wi[:, OFF_KR:OFF_RW]
    uq = w_uq[layer].reshape(Q_LORA, MLA_HEADS, QK_NOPE + QK_ROPE)
    uq_pad = jnp.zeros((Q_LORA, MLA_HEADS, QK_PAD - QK_NOPE - QK_ROPE), F32)
    wq_full = jnp.concatenate([uq, uq_pad], -1).reshape(Q_LORA, MLA_HEADS * QK_PAD)
    wq_sw = jnp.concatenate([jnp.zeros((Q_LORA, MLA_HEADS, QK_NOPE), F32), _swap_halves(uq[..., QK_NOPE:]), uq_pad],
                            -1).reshape(Q_LORA, MLA_HEADS * QK_PAD)
    uk = w_uk[layer].reshape(KV_LORA, MLA_HEADS, QK_NOPE)
    wk_full = jnp.concatenate([uk, jnp.zeros((KV_LORA, MLA_HEADS, QK_PAD - QK_NOPE), F32)], -1)
    e_head = jnp.concatenate([jnp.zeros((QK_ROPE, QK_NOPE), F32), jnp.eye(QK_ROPE, dtype=F32),
                              jnp.zeros((QK_ROPE, QK_PAD - QK_NOPE - QK_ROPE), F32)], -1)
    lora_pad = jnp.zeros((DECAY_LORA, RWKV_WIDTH), F32)
    row = lambda vec: vec[layer].reshape(1, -1)
    head_of = jnp.arange(2 * LANES) // RWKV_HEAD
    tri_lo = jnp.tril(jnp.ones((CHUNK, CHUNK), F32))
    rwkv = {
        "mu": row(tok_shift_mu), "k_k": row(k_k), "k_a": row(k_a), "r_k": row(r_k),
        "gn_g": row(gn_g), "gn_b": row(gn_b), "g_up": g_up[layer].astype(BF16),
        "dirs": [
            {"w0": row(w0_fwd), "a0": row(a0_fwd),
             "w_up": jnp.concatenate([w_up_fwd[layer], lora_pad], 0).astype(BF16),
             "a_up": jnp.concatenate([lora_pad, a_up_fwd[layer]], 0).astype(BF16)},
            {"w0": row(w0_bwd), "a0": row(a0_bwd),
             "w_up": jnp.concatenate([w_up_bwd[layer], lora_pad], 0).astype(BF16),
             "a_up": jnp.concatenate([lora_pad, a_up_bwd[layer]], 0).astype(BF16)}],
        "ones_bd": (head_of[:, None] == head_of[None, :]).astype(BF16),
        "tri": [tri_lo.astype(BF16), tri_lo.T.astype(BF16)],
    }
    wts = {
        "w_mla": jnp.concatenate([wi[:, :OFF_RW], _swap_halves(kr_w)], -1),
        "w_rw": wi[:, OFF_RW:],
        "q_norm_g": q_norm_g[layer], "kv_norm_g": kv_norm_g[layer],
        "wq_full": wq_full.astype(BF16), "wq_sw": wq_sw.astype(BF16),
        "wk_full": wk_full.reshape(KV_LORA, MLA_HEADS * QK_PAD).astype(BF16),
        "w_uv": w_uv[layer].astype(BF16),
        "e_mat": jnp.tile(e_head, (1, MLA_HEADS)).astype(BF16),
        "rwkv": rwkv,
        "w_out_att": w_out[layer][:MLA_HEADS * V_HEAD].astype(BF16),
        "w_out_rw": w_out[layer][MLA_HEADS * V_HEAD:].astype(BF16),
        "ln1_g": ln1_g[layer], "ln1_b": ln1_b[layer],
        "w_ffn_gate": w_ffn_gate[layer].astype(BF16), "w_ffn_up": w_ffn_up[layer].astype(BF16),
        "w_ffn_down": w_ffn_down[layer].astype(BF16),
        "ln2_g": ln2_g[layer], "ln2_b": ln2_b[layer],
    }

    y_prompt, kv_ctx, s_f, s_b = _layer(
        x_prompt.reshape(batch * seq, D_MODEL), batch, seq, mod4, dec_batch, False, wts, None, None, None, True)
    past = cache_ckv.shape[2]
    cache = (cache_ckv[:, layer].reshape(dec_batch * past, KV_LORA),
             cache_krope[:, layer].reshape(dec_batch * past, QK_ROPE))
    y_sample, _, _, _ = _layer(
        x_sample.reshape(dec_batch * dec_seq, D_MODEL), dec_batch, dec_seq, mod4, 0, True, wts,
        _rope_tables(dec_seq), cache, (state_wkv_fwd[:, layer], state_wkv_bwd[:, layer]), False)

    return (y_prompt.reshape(batch, seq, D_MODEL),
            y_sample.reshape(dec_batch, dec_seq, D_MODEL),
            kv_ctx[2].reshape(batch, 1, seq, KV_LORA),
            kv_ctx[3].reshape(batch, 1, seq, QK_ROPE),
            _unblock_diag_state(s_f)[:, None],
            _unblock_diag_state(s_b)[:, None])
```

```python
import functools
import math

import jax
import jax.numpy as jnp
from jax import lax
from jax.experimental import pallas as pl
from jax.experimental.pallas import tpu as pltpu

F32 = jnp.float32
BF16 = jnp.bfloat16

D_MODEL = 2048
GRID_W = 64
MLA_HEADS = 8
QK_NOPE = 128
QK_ROPE = 64
V_HEAD = 128
Q_LORA = 512
KV_LORA = 256
ROPE_AXIS_FREQS = QK_ROPE // 4
ROPE_THETA = 10000.0
RWKV_HEADS = 16
RWKV_HEAD = 64
RWKV_WIDTH = RWKV_HEADS * RWKV_HEAD
DECAY_LORA = 64
ICLR_LORA = 64
GATE_LORA = 128
OFF_KV = Q_LORA
OFF_KR = OFF_KV + KV_LORA
OFF_RW = OFF_KR + QK_ROPE
RW_COLS = 3 * RWKV_WIDTH + DECAY_LORA + ICLR_LORA + GATE_LORA
D_FF = 5632
LN_EPS = 1e-5
RMS_EPS = 1e-6
GN_EPS = 64e-5
DEPTH = 1
ALPHA = (2.0 * DEPTH) ** 0.25

LANES = 128
QK_PAD = 2 * LANES
MLA_COLS = OFF_RW + QK_ROPE
CHUNK = 64
HEAD_PAIRS = RWKV_HEADS // 2
ATTN_TQ = 512
ATTN_HEADS = 4
IN_TM = 512
OUT_TM = 512
FFN_TM = 512
FFN_TF = 512
SCAN_TM = 512
SCAN_GROUP = 2
SCAN_PAIRS = 8
VMEM_LIMIT = 56 * 1024 * 1024


def _cp(sem):
    return pltpu.CompilerParams(dimension_semantics=sem, vmem_limit_bytes=VMEM_LIMIT)


def _dot(a, b):
    return jnp.dot(a, b, preferred_element_type=F32)


def _dot_nt(a, b):
    return lax.dot_general(a, b, (((1,), (1,)), ((), ())), preferred_element_type=F32)


def _dot_tn(a, b):
    return lax.dot_general(a, b, (((0,), (0,)), ((), ())), preferred_element_type=F32)


def _sigmoid(x):
    return 1.0 / (1.0 + jnp.exp(-x))


def _layer_norm(h, g, b):
    mu = jnp.mean(h, -1, keepdims=True)
    d = h - mu
    var = jnp.mean(d * d, -1, keepdims=True)
    return d * lax.rsqrt(var + LN_EPS) * g + b


def _sigmoid_tanh(x):
    return 0.5 * jnp.tanh(0.5 * x) + 0.5


def _split2(x):
    hi = x.astype(BF16)
    return hi, (x - hi.astype(F32)).astype(BF16)


def _dot2_right(x, m):
    hi, lo = _split2(x)
    return _dot(hi, m) + _dot(lo, m)


def _dot2_left(m, x):
    hi, lo = _split2(x)
    return _dot(m, hi) + _dot(m, lo)


def _mod_kernel(c_ref, w_ref, b_ref, o_ref):
    c = c_ref[...]
    s = (c * _sigmoid(c)).astype(BF16)
    o_ref[...] = _dot(s, w_ref[...].astype(BF16)) + b_ref[...]


def _modulation(cond, w_mod, b_mod):
    rows, tn = cond.shape[0], 1024
    n = w_mod.shape[1]
    return pl.pallas_call(
        _mod_kernel,
        grid=(n // tn,),
        in_specs=[pl.BlockSpec((rows, D_MODEL), lambda j: (0, 0)),
                  pl.BlockSpec((D_MODEL, tn), lambda j: (0, j)),
                  pl.BlockSpec((1, tn), lambda j: (0, j))],
        out_specs=pl.BlockSpec((rows, tn), lambda j: (0, j)),
        out_shape=jax.ShapeDtypeStruct((rows, n), F32),
        compiler_params=_cp(("arbitrary",)),
        name="modulation",
    )(cond, w_mod, b_mod.reshape(1, n))


def _mod_spec(which, seq_of):
    return pl.BlockSpec((None, None, 1, D_MODEL), lambda *g: (seq_of(*g), which, 0, 0))


def _in_kernel(x_ref, sh_ref, sc_ref, wm_ref, wr_ref, om_ref, or_ref):
    xm = (x_ref[...] * (1.0 + sc_ref[...]) + sh_ref[...]).astype(BF16)
    om_ref[...] = _dot(xm, wm_ref[...])
    or_ref[...] = _dot(xm, wr_ref[...])


def _in_proj(x2, mod4, seq0, n_seq_tokens, w_mla, w_rw):
    t, tm = x2.shape[0], IN_TM
    per_seq = n_seq_tokens // tm
    seq_of = (lambda i: seq0) if per_seq == 0 else (lambda i: seq0 + i // per_seq)
    resident = pl.Buffered(1)
    return pl.pallas_call(
        _in_kernel,
        grid=(t // tm,),
        in_specs=[pl.BlockSpec((tm, D_MODEL), lambda i: (i, 0)),
                  _mod_spec(0, seq_of), _mod_spec(1, seq_of),
                  pl.BlockSpec((D_MODEL, MLA_COLS), lambda i: (0, 0), pipeline_mode=resident),
                  pl.BlockSpec((D_MODEL, RW_COLS), lambda i: (0, 0), pipeline_mode=resident)],
        out_specs=[pl.BlockSpec((tm, MLA_COLS), lambda i: (i, 0)),
                   pl.BlockSpec((tm, RW_COLS), lambda i: (i, 0))],
        out_shape=[jax.ShapeDtypeStruct((t, MLA_COLS), F32),
                   jax.ShapeDtypeStruct((t, RW_COLS), F32)],
        compiler_params=_cp(("arbitrary",)),
        name="in_proj",
    )(x2, mod4, mod4, w_mla, w_rw)


def _q_kernel(*refs, rope):
    if rope:
        qd_ref, g_ref, w_ref, wsw_ref, c_ref, s_ref, o_ref = refs
    else:
        qd_ref, g_ref, w_ref, o_ref = refs
    x = qd_ref[...]
    qn = (x * lax.rsqrt(jnp.mean(x * x, -1, keepdims=True) + RMS_EPS) * g_ref[...]).astype(BF16)
    q = _dot(qn, w_ref[...])
    if rope:
        qs = _dot(qn, wsw_ref[...])
        c, s = c_ref[...], s_ref[...]
        parts = []
        for h in range(MLA_HEADS):
            lo = h * QK_PAD
            parts += [q[:, lo:lo + LANES],
                      q[:, lo + LANES:lo + QK_PAD] * c + qs[:, h * LANES:(h + 1) * LANES] * s]
        q = jnp.concatenate(parts, axis=1)
    o_ref[...] = q.astype(BF16)


def _q_proj(proj_mla, n_seq_tokens, q_norm_g, wq_full, wq_sw, rope_tabs):
    t, tm = proj_mla.shape[0], 256
    per_seq = n_seq_tokens // tm
    rope = rope_tabs is not None
    width = MLA_HEADS * QK_PAD
    const = lambda i: (0, 0)
    in_specs = [pl.BlockSpec((tm, Q_LORA), lambda i: (i, 0)),
                pl.BlockSpec((1, Q_LORA), const),
                pl.BlockSpec((Q_LORA, width), const)]
    args = [proj_mla, q_norm_g.reshape(1, Q_LORA), wq_full]
    if rope:
        in_specs += [pl.BlockSpec((Q_LORA, MLA_HEADS * LANES), const),
                     pl.BlockSpec((tm, LANES), lambda i: (i % per_seq, 0)),
                     pl.BlockSpec((tm, LANES), lambda i: (i % per_seq, 0))]
        args += [wq_sw, rope_tabs[0], rope_tabs[1]]
    return pl.pallas_call(
        functools.partial(_q_kernel, rope=rope),
        grid=(t // tm,),
        in_specs=in_specs,
        out_specs=pl.BlockSpec((tm, width), lambda i: (i, 0)),
        out_shape=jax.ShapeDtypeStruct((t, width), BF16),
        compiler_params=_cp(("arbitrary",)),
        name="q_proj",
    )(*args)


def _kv_kernel(*refs, norm, rope, emit_new):
    it = iter(refs)
    ckv_ref, kr_ref = next(it), next(it)
    g_ref = next(it) if norm else None
    cs_ref = next(it) if rope else None
    wk_ref, wv_ref, e_ref = next(it), next(it), next(it)
    k_out, v_out = next(it), next(it)
    x = ckv_ref[...]
    if norm:
        x = x * lax.rsqrt(jnp.mean(x * x, -1, keepdims=True) + RMS_EPS) * g_ref[...]
    xb = x.astype(BF16)
    kr = kr_ref[...]
    if rope:
        y = kr * cs_ref[...]
        kr = y + pltpu.roll(y, QK_ROPE, 1)
    kr = kr[:, :QK_ROPE]
    k_out[...] = (_dot(xb, wk_ref[...]) + _dot(kr.astype(BF16), e_ref[...])).T.astype(BF16)
    v_out[...] = _dot(xb, wv_ref[...]).astype(BF16)
    if emit_new:
        nc_out, nk_out = next(it), next(it)
        nc_out[...] = x
        nk_out[...] = kr


def _kv_proj(ckv_src, ckv_blk, kr_src, kr_blk, kr_w, n_seq_tokens, kv_norm_g, cs_tab, wk_full, wv, e_mat, emit_new):
    t, tm = ckv_src.shape[0], 256
    per_seq = n_seq_tokens // tm
    norm, rope = kv_norm_g is not None, cs_tab is not None
    kw, vw = MLA_HEADS * QK_PAD, MLA_HEADS * V_HEAD
    const = lambda i: (0, 0)
    in_specs = [pl.BlockSpec((tm, KV_LORA), lambda i: (i, ckv_blk)),
                pl.BlockSpec((tm, kr_w), lambda i: (i, kr_blk))]
    args = [ckv_src, kr_src]
    if norm:
        in_specs.append(pl.BlockSpec((1, KV_LORA), const))
        args.append(kv_norm_g.reshape(1, KV_LORA))
    if rope:
        in_specs.append(pl.BlockSpec((tm, LANES), lambda i: (i % per_seq, 0)))
        args.append(cs_tab)
    in_specs += [pl.BlockSpec((KV_LORA, kw), const), pl.BlockSpec((KV_LORA, vw), const),
                 pl.BlockSpec((QK_ROPE, kw), const)]
    args += [wk_full, wv, e_mat]
    out_specs = [pl.BlockSpec((kw, tm), lambda i: (i // per_seq, i % per_seq)),
                 pl.BlockSpec((tm, vw), lambda i: (i, 0))]
    out_shape = [jax.ShapeDtypeStruct((t // n_seq_tokens * kw, n_seq_tokens), BF16),
                 jax.ShapeDtypeStruct((t, vw), BF16)]
    if emit_new:
        out_specs += [pl.BlockSpec((tm, KV_LORA), lambda i: (i, 0)), pl.BlockSpec((tm, QK_ROPE), lambda i: (i, 0))]
        out_shape += [jax.ShapeDtypeStruct((t, KV_LORA), F32), jax.ShapeDtypeStruct((t, QK_ROPE), F32)]
    return pl.pallas_call(
        functools.partial(_kv_kernel, norm=norm, rope=rope, emit_new=emit_new),
        grid=(t // tm,),
        in_specs=in_specs, out_specs=out_specs, out_shape=out_shape,
        compiler_params=_cp(("arbitrary",)),
        name="kv_proj",
    )(*args)


def _attn_kernel(*refs, cache, heads, exp2_scale):
    if cache:
        q_ref, k_ref, v_ref, kc_ref, vc_ref, o_ref = refs
    else:
        q_ref, k_ref, v_ref, o_ref = refs
    out = [None] * heads

    def head_stages(h):
        qk = slice(h * QK_PAD, (h + 1) * QK_PAD)
        hv = slice(h * V_HEAD, (h + 1) * V_HEAD)
        q = q_ref[:, qk]
        s = _dot(q, k_ref[qk, :])
        sc = _dot(q, kc_ref[qk, :]) if cache else None
        yield
        m = jnp.max(s, -1, keepdims=True)
        if cache:
            m = jnp.maximum(m, jnp.max(sc, -1, keepdims=True))
        yield
        p = jnp.exp2((s - m) * exp2_scale)
        l = jnp.sum(p, -1, keepdims=True)
        if cache:
            pc = jnp.exp2((sc - m) * exp2_scale)
            l = l + jnp.sum(pc, -1, keepdims=True)
        yield
        acc = _dot(p.astype(BF16), v_ref[:, hv])
        if cache:
            acc = acc + _dot(pc.astype(BF16), vc_ref[:, hv])
        out[h] = acc * (1.0 / l)
        yield

    live, waiting = [], [head_stages(h) for h in range(heads)]
    while live or waiting:
        if waiting:
            live.append(waiting.pop(0))
        live = [g for g in live if next(g, StopIteration) is not StopIteration]
    o_ref[...] = jnp.concatenate(out, axis=1).astype(BF16)


def _attention(q, kt, v, kct, vc, batch, n_tok):
    tq, heads = min(ATTN_TQ, n_tok), ATTN_HEADS
    n_q = n_tok // tq
    groups = MLA_HEADS // heads
    cache = kct is not None
    in_specs = [pl.BlockSpec((tq, heads * QK_PAD), lambda b, h, i: (b * n_q + i, h)),
                pl.BlockSpec((heads * QK_PAD, n_tok), lambda b, h, i: (b * groups + h, 0)),
                pl.BlockSpec((n_tok, heads * V_HEAD), lambda b, h, i: (b, h))]
    args = [q, kt, v]
    if cache:
        n_c = vc.shape[0] // batch
        in_specs += [pl.BlockSpec((heads * QK_PAD, n_c), lambda b, h, i: (b * groups + h, 0)),
                     pl.BlockSpec((n_c, heads * V_HEAD), lambda b, h, i: (b, h))]
        args += [kct, vc]
    scale = 1.0 / math.sqrt(QK_NOPE + QK_ROPE)
    return pl.pallas_call(
        functools.partial(_attn_kernel, cache=cache, heads=heads, exp2_scale=scale * math.log2(math.e)),
        grid=(batch, groups, n_q),
        in_specs=in_specs,
        out_specs=pl.BlockSpec((tq, heads * V_HEAD), lambda b, h, i: (b * n_q + i, h)),
        out_shape=jax.ShapeDtypeStruct((batch * n_tok, MLA_HEADS * V_HEAD), BF16),
        compiler_params=_cp(("arbitrary", "arbitrary", "arbitrary")),
        name="attention",
    )(*args)


def _pair_blockdiag(y, lane_lo):
    zero = jnp.zeros_like(y)
    return jnp.concatenate([jnp.where(lane_lo, y, zero), jnp.where(lane_lo, zero, y)], axis=0)


def _scan_kernel(*refs, tm, npair, reverse, zero_init, emit_state, finalize):
    it = iter(refs)
    blocks = {}
    cols = ("r", "k", "v", "wa") + (("gd",) if finalize else ())
    for name in cols:
        blocks[name] = (next(it), next(it), next(it), next(it))
    kk_ref, ka_ref, w0_ref, wup_ref, a0_ref, aup_ref = (next(it) for _ in range(6))
    if finalize:
        rk_ref, a0o_ref, aupo_ref, gup_ref, gng_ref, gnb_ref, yo_ref = (next(it) for _ in range(7))
    ones_ref, tri_ref = next(it), next(it)
    s0_ref = None if zero_init else next(it)
    y_ref = next(it)
    sout_ref = next(it) if emit_state else None
    s_scr = next(it)

    t_idx = pl.program_id(2)
    n_t = pl.num_programs(2)
    tile = (n_t - 1 - t_idx) if reverse else t_idx
    first_tok = tile == 0
    last_tok = tile == n_t - 1

    @pl.when(t_idx == 0)
    def _():
        if zero_init:
            s_scr[...] = jnp.zeros_like(s_scr)
        else:
            s_scr[...] = s0_ref[...]

    def shifted(name):
        main, prv, nxt, mu = blocks[name]
        u = main[...]
        row = lax.broadcasted_iota(jnp.int32, u.shape, 0)
        p_row = jnp.where(first_tok, 0.0, prv[7:8, :])
        n_row = jnp.where(last_tok, 0.0, nxt[0:1, :])
        up = jnp.where(row == 0, p_row, pltpu.roll(u, 1, 0))
        un = jnp.where(row == tm - 1, n_row, pltpu.roll(u, tm - 1, 0))
        return u + mu[...] * (0.5 * (up + un) - u)

    ones_bd = ones_ref[...]

    seg_w = ones_bd.shape[0]

    def seg(x):
        return jnp.concatenate(
            [_dot2_right(x[:, j * seg_w:(j + 1) * seg_w], ones_bd) for j in range(x.shape[1] // seg_w)], axis=1)

    r, k, v, wa = shifted("r"), shifted("k"), shifted("v"), shifted("wa")
    wab = wa.astype(BF16)
    kk = k * kk_ref[...]
    kk = kk * lax.rsqrt(jnp.maximum(seg(kk * kk), 1e-24))
    ld = _sigmoid_tanh(w0_ref[...] + _dot(jnp.tanh(wa).astype(BF16), wup_ref[...])) * (-math.exp(-0.5))
    a = _sigmoid_tanh(a0_ref[...] + _dot(wab, aup_ref[...]))
    kd = k * (1.0 + (a - 1.0) * ka_ref[...])
    b = kk * a

    c_row = lax.broadcasted_iota(jnp.int32, (CHUNK, LANES), 0)
    c_lane = lax.broadcasted_iota(jnp.int32, (CHUNK, LANES), 1)
    c_src = c_lane & (CHUNK - 1)
    lane_lo = c_lane < CHUNK
    strict = (c_src > c_row) if reverse else (c_src < c_row)
    incl = (c_src >= c_row) if reverse else (c_src <= c_row)
    eye = (c_src == c_row).astype(F32)
    bd_row = lax.broadcasted_iota(jnp.int32, (LANES, LANES), 0)
    bd_lane = lax.broadcasted_iota(jnp.int32, (LANES, LANES), 1)
    bd_mask = jnp.right_shift(bd_row, 6) == jnp.right_shift(bd_lane, 6)
    tri = tri_ref[...]

    def nn(p, y):
        return _dot(p.astype(BF16), _pair_blockdiag(y.astype(BF16), lane_lo))

    n_chunks = tm // CHUNK
    order = list(range(n_chunks - 1, -1, -1) if reverse else range(n_chunks))
    cums = [_dot2_left(tri, ld[c * CHUNK:(c + 1) * CHUNK]) for c in range(n_chunks)]
    pairs = range(npair)
    strict2 = jnp.concatenate([strict, strict], axis=1)
    incl2 = jnp.concatenate([incl, incl], axis=1)
    st = {}

    def independent_stage(c):
        sl = slice(c * CHUNK, (c + 1) * CHUNK)
        for j in pairs:
            ln = slice(j * LANES, (j + 1) * LANES)
            r_c, v_c, kk_c, kd_c, b_c, ld_c = r[sl, ln], v[sl, ln], kk[sl, ln], kd[sl, ln], b[sl, ln], ld[sl, ln]
            cum = cums[c][:, ln]
            tot = cum[0:1] if reverse else cum[CHUNK - 1:CHUNK]
            e_neg = jnp.exp(-cum)
            e_tot = jnp.exp(tot - cum)
            a_hat = -kk_c * jnp.exp(cum - ld_c)
            r_hat = r_c * jnp.exp(cum)
            ar = jnp.concatenate([a_hat, r_hat], axis=0).astype(BF16)
            bk_hat = jnp.concatenate([_pair_blockdiag((b_c * e_neg).astype(BF16), lane_lo),
                                      _pair_blockdiag((kd_c * e_neg).astype(BF16), lane_lo)], axis=0)
            g = _dot_nt(ar, bk_hat)
            gs = jnp.where(strict2, g[:CHUNK], 0.0)
            lmat = gs[:, :LANES]
            st[j, c] = dict(
                ar=ar, v=v_c.astype(BF16), v_bd=_pair_blockdiag(v_c.astype(BF16), lane_lo),
                p=eye + lmat, q=lmat, akm=gs[:, LANES:],
                rbk=jnp.where(incl2, g[CHUNK:], 0.0).astype(BF16),
                bk=jnp.concatenate([b_c * e_tot, kd_c * e_tot], axis=0).astype(BF16),
                w=jnp.exp(tot))
        yield
        for j in pairs:
            st[j, c]["q"] = nn(st[j, c]["q"], st[j, c]["q"])
        yield
        for _ in range(int(math.log2(CHUNK)) - 2):
            for j in pairs:
                d = st[j, c]
                pq = nn(jnp.concatenate([d["p"], d["q"]], axis=0), d["q"])
                d["p"], d["q"] = d["p"] + pq[:CHUNK], pq[CHUNK:]
            yield
        for j in pairs:
            d = st[j, c]
            d["p"] = d["p"] + nn(d["p"], d["q"])
            d["akv"] = _dot(d["akm"].astype(BF16), d["v_bd"])
        yield

    s_cur = [s_scr[j] for j in pairs]
    y_cells = {}

    def dependent_stage(c):
        xs = [_dot_nt(st[j, c]["ar"], s_cur[j].astype(BF16)) for j in pairs]
        yield
        u = [nn(st[j, c]["p"], xs[j][:CHUNK] + st[j, c]["akv"]) for j in pairs]
        yield
        for j in pairs:
            d = st[j, c]
            y_cells[j, c] = xs[j][CHUNK:] + _dot(
                d["rbk"],
                jnp.concatenate([_pair_blockdiag(u[j].astype(BF16), lane_lo), d["v_bd"]], axis=0))
            upd = _dot_tn(jnp.concatenate([u[j].astype(BF16), d["v"]], axis=0), d["bk"])
            s_cur[j] = s_cur[j] * d["w"] + jnp.where(bd_mask, upd, 0.0)
        yield

    def run_side_by_side(*stages):
        live = list(stages)
        while live:
            for g in list(live):
                if next(g, StopIteration) is StopIteration:
                    live.remove(g)

    def in_sequence(stages):
        for g in stages:
            yield from g

    groups = [order[i:i + SCAN_GROUP] for i in range(0, n_chunks, SCAN_GROUP)]
    for gi in range(len(groups) + 1):
        stages = [independent_stage(c) for c in groups[gi]] if gi < len(groups) else []
        if gi > 0:
            stages.append(in_sequence([dependent_stage(c) for c in groups[gi - 1]]))
        run_side_by_side(*stages)
    s_new = s_cur
    for j in range(npair):
        s_scr[j] = s_new[j]
    y = jnp.concatenate(
        [jnp.concatenate([y_cells[j, c] for c in range(n_chunks)], axis=0) for j in range(npair)], axis=1)

    if emit_state:
        @pl.when(t_idx == n_t - 1)
        def _():
            for j in range(npair):
                sout_ref[j] = s_new[j]

    if not finalize:
        y_ref[...] = y
    else:
        rk = rk_ref[...]
        a_o = _sigmoid_tanh(a0o_ref[...] + _dot(wab, aupo_ref[...]))
        kd_o = k * (1.0 + (a_o - 1.0) * ka_ref[...])
        bonus = seg(r * rk * (kd + kd_o))
        gate = _dot(_sigmoid_tanh(shifted("gd")).astype(BF16), gup_ref[...])
        yt = y + yo_ref[...]
        inv_n = 1.0 / RWKV_HEAD
        d = yt - seg(yt) * inv_n
        var = seg(d * d) * inv_n
        yn = d * lax.rsqrt(var + GN_EPS) * gng_ref[...] + gnb_ref[...]
        y_ref[...] = ((yn + bonus * v) * gate).astype(BF16)


def _scan(proj_rw, batch, n_tok, prm, direction, s0_bd, emit_state, y_other):
    reverse = direction == 1
    finalize = y_other is not None
    zero_init = s0_bd is None
    t = batch * n_tok
    tm = min(SCAN_TM, n_tok)
    npair = SCAN_PAIRS
    width = npair * LANES
    groups = HEAD_PAIRS // npair
    n_t = n_tok // tm
    row8 = tm // 8
    n_row8 = t // 8

    def tile_of(ti):
        return (n_t - 1 - ti) if reverse else ti

    def token_specs(w, cb):
        return [pl.BlockSpec((tm, w), lambda bb, p, ti: (bb * n_t + tile_of(ti), cb(p))),
                pl.BlockSpec((8, w), lambda bb, p, ti: (jnp.maximum((bb * n_t + tile_of(ti)) * row8 - 1, 0), cb(p))),
                pl.BlockSpec((8, w), lambda bb, p, ti: (jnp.minimum((bb * n_t + tile_of(ti) + 1) * row8, n_row8 - 1),
                                                        cb(p))),
                pl.BlockSpec((1, w), lambda bb, p, ti: (0, cb(p)))]

    rowvec = lambda: pl.BlockSpec((1, width), lambda bb, p, ti: (0, p))
    lora = lambda: pl.BlockSpec((LANES, width), lambda bb, p, ti: (0, p))
    const = lambda shape: pl.BlockSpec(shape, lambda bb, p, ti: (0, 0))
    state = lambda: pl.BlockSpec((None, npair, LANES, LANES), lambda bb, p, ti: (bb, p, 0, 0))
    y_spec = lambda: pl.BlockSpec((tm, width), lambda bb, p, ti: (bb * n_t + tile_of(ti), p))

    lora_blk = 3 * RWKV_WIDTH // LANES
    in_specs, args = [], []
    for name, w, cb in (("r", width, lambda p: p), ("k", width, lambda p: groups + p),
                        ("v", width, lambda p: 2 * groups + p), ("wa", LANES, lambda p: lora_blk),
                        ("gd", LANES, lambda p: lora_blk + 1)):
        if name == "gd" and not finalize:
            continue
        in_specs += token_specs(w, cb)
        args += [proj_rw, proj_rw, proj_rw, prm["mu"]]
    d, o = prm["dirs"][direction], prm["dirs"][1 - direction]
    in_specs += [rowvec(), rowvec(), rowvec(), lora(), rowvec(), lora()]
    args += [prm["k_k"], prm["k_a"], d["w0"], d["w_up"], d["a0"], d["a_up"]]
    if finalize:
        in_specs += [rowvec(), rowvec(), lora(), lora(), rowvec(), rowvec(), y_spec()]
        args += [prm["r_k"], o["a0"], o["a_up"], prm["g_up"], prm["gn_g"], prm["gn_b"], y_other]
    in_specs += [const(prm["ones_bd"].shape), const((CHUNK, CHUNK))]
    args += [prm["ones_bd"], prm["tri"][direction]]
    if not zero_init:
        in_specs.append(state())
        args.append(s0_bd)
    out_specs = [y_spec()]
    out_shape = [jax.ShapeDtypeStruct((t, RWKV_WIDTH), BF16 if finalize else F32)]
    if emit_state:
        out_specs.append(state())
        out_shape.append(jax.ShapeDtypeStruct((batch, HEAD_PAIRS, LANES, LANES), F32))
    out = pl.pallas_call(
        functools.partial(_scan_kernel, tm=tm, npair=npair, reverse=reverse, zero_init=zero_init,
                          emit_state=emit_state, finalize=finalize),
        grid=(batch, groups, n_t),
        in_specs=in_specs, out_specs=out_specs, out_shape=out_shape,
        scratch_shapes=[pltpu.VMEM((npair, LANES, LANES), F32)],
        compiler_params=_cp(("arbitrary", "arbitrary", "arbitrary")),
        name="rwkv_scan_bwd" if reverse else "rwkv_scan_fwd",
    )(*args)
    return out if emit_state else (out[0], None)


def _out_kernel(att_ref, rw_ref, x_ref, g_ref, wa_ref, wr_ref, lg_ref, lb_ref, o_ref):
    half = o_ref.shape[0] // 2
    for rows in (slice(0, half), slice(half, 2 * half)):
        f = _dot(att_ref[rows, :], wa_ref[...]) + _dot(rw_ref[rows, :], wr_ref[...])
        o_ref[rows, :] = _layer_norm(ALPHA * x_ref[rows, :] + g_ref[...] * f, lg_ref[...], lb_ref[...])


def _out_proj(att, rw, x2, mod4, seq0, n_seq_tokens, w_att, w_rw, ln_g, ln_b):
    t, tm = x2.shape[0], min(OUT_TM, n_seq_tokens)
    per_seq = n_seq_tokens // tm
    seq_of = lambda i: seq0 + i // per_seq
    half = D_MODEL // 2
    const = lambda i: (0, 0)
    return pl.pallas_call(
        _out_kernel,
        grid=(t // tm,),
        in_specs=[pl.BlockSpec((tm, half), lambda i: (i, 0)), pl.BlockSpec((tm, half), lambda i: (i, 0)),
                  pl.BlockSpec((tm, D_MODEL), lambda i: (i, 0)), _mod_spec(2, seq_of),
                  pl.BlockSpec((half, D_MODEL), const), pl.BlockSpec((half, D_MODEL), const),
                  pl.BlockSpec((1, D_MODEL), const), pl.BlockSpec((1, D_MODEL), const)],
        out_specs=pl.BlockSpec((tm, D_MODEL), lambda i: (i, 0)),
        out_shape=jax.ShapeDtypeStruct((t, D_MODEL), F32),
        compiler_params=_cp(("arbitrary",)),
        name="out_proj",
    )(att, rw, x2, mod4, w_att, w_rw, ln_g.reshape(1, D_MODEL), ln_b.reshape(1, D_MODEL))


def _ffn_kernel(x_ref, sh_ref, sc_ref, g_ref, wg_ref, wu_ref, wd_ref, lg_ref, lb_ref, o_ref, h_scr, acc_scr):
    kf = pl.program_id(1)

    @pl.when(kf == 0)
    def _():
        h_scr[...] = (x_ref[...] * (1.0 + sc_ref[...]) + sh_ref[...]).astype(BF16)
        acc_scr[...] = jnp.zeros_like(acc_scr)

    h = h_scr[...]
    gate = _dot(h, wg_ref[...])
    act = (gate * _sigmoid(gate) * _dot(h, wu_ref[...])).astype(BF16)
    acc_scr[...] += _dot(act, wd_ref[...])

    @pl.when(kf == pl.num_programs(1) - 1)
    def _():
        o_ref[...] = _layer_norm(ALPHA * x_ref[...] + g_ref[...] * acc_scr[...], lg_ref[...], lb_ref[...])


def _ffn(x1, mod4, seq0, n_seq_tokens, w_gate, w_up, w_down, ln_g, ln_b):
    t, tf = x1.shape[0], FFN_TF
    tm = min(FFN_TM, n_seq_tokens)
    per_seq = n_seq_tokens // tm
    seq_of = lambda i, kf: seq0 + i // per_seq
    const = lambda i, kf: (0, 0)
    return pl.pallas_call(
        _ffn_kernel,
        grid=(t // tm, D_FF // tf),
        in_specs=[pl.BlockSpec((tm, D_MODEL), lambda i, kf: (i, 0)),
                  _mod_spec(3, seq_of), _mod_spec(4, seq_of), _mod_spec(5, seq_of),
                  pl.BlockSpec((D_MODEL, tf), lambda i, kf: (0, kf)),
                  pl.BlockSpec((D_MODEL, tf), lambda i, kf: (0, kf)),
                  pl.BlockSpec((tf, D_MODEL), lambda i, kf: (kf, 0)),
                  pl.BlockSpec((1, D_MODEL), const), pl.BlockSpec((1, D_MODEL), const)],
        out_specs=pl.BlockSpec((tm, D_MODEL), lambda i, kf: (i, 0)),
        out_shape=jax.ShapeDtypeStruct((t, D_MODEL), F32),
        scratch_shapes=[pltpu.VMEM((tm, D_MODEL), BF16), pltpu.VMEM((tm, D_MODEL), F32)],
        compiler_params=_cp(("arbitrary", "arbitrary")),
        name="ffn",
    )(x1, mod4, mod4, mod4, w_gate, w_up, w_down, ln_g.reshape(1, D_MODEL), ln_b.reshape(1, D_MODEL))


def _rope_tables(n_tokens):
    rows = n_tokens // GRID_W
    row = jnp.repeat(jnp.arange(rows), GRID_W).astype(F32)
    col = jnp.tile(jnp.arange(GRID_W), rows).astype(F32)
    freqs = ROPE_THETA ** (-jnp.arange(ROPE_AXIS_FREQS, dtype=F32) / ROPE_AXIS_FREQS)
    ang = jnp.concatenate([row[:, None] * freqs, col[:, None] * freqs], -1)
    cos, sin = jnp.cos(ang), jnp.sin(ang)
    cos2 = jnp.concatenate([cos, cos], -1)
    sin2 = jnp.concatenate([-sin, sin], -1)
    pad = jnp.zeros((n_tokens, LANES - QK_ROPE), F32)
    q_cos = jnp.concatenate([cos2, pad], -1)
    q_sin = jnp.concatenate([sin2, pad], -1)
    k_tab = jnp.concatenate([cos2, sin2], -1)
    return q_cos, q_sin, k_tab


def _swap_halves(w):
    half = w.shape[-1] // 2
    return jnp.concatenate([w[..., half:], w[..., :half]], -1)


def _block_diag_state(s):
    bsz = s.shape[0]
    s = s.reshape(bsz, HEAD_PAIRS, 2, RWKV_HEAD, RWKV_HEAD)
    z = jnp.zeros_like(s[:, :, 0])
    top = jnp.concatenate([s[:, :, 0], z], -1)
    bot = jnp.concatenate([z, s[:, :, 1]], -1)
    return jnp.concatenate([top, bot], -2)


def _unblock_diag_state(s_bd):
    bsz = s_bd.shape[0]
    s = jnp.stack([s_bd[:, :, :RWKV_HEAD, :RWKV_HEAD], s_bd[:, :, RWKV_HEAD:, RWKV_HEAD:]], 2)
    return s.reshape(bsz, RWKV_HEADS, RWKV_HEAD, RWKV_HEAD)


def _layer(x2, batch, n_tok, mod4, seq0, per_seq_mod, wts, rope_tabs, cache, states, is_context):
    mod_tokens = n_tok if per_seq_mod else 0
    proj_mla, proj_rw = _in_proj(x2, mod4, seq0, mod_tokens, wts["w_mla"], wts["w_rw"])
    q = _q_proj(proj_mla, n_tok, wts["q_norm_g"], wts["wq_full"], wts["wq_sw"],
                None if rope_tabs is None else rope_tabs[:2])
    kv = _kv_proj(proj_mla, OFF_KV // KV_LORA, proj_mla, OFF_KR // LANES, LANES, n_tok, wts["kv_norm_g"],
                  None if rope_tabs is None else rope_tabs[2], wts["wk_full"], wts["w_uv"], wts["e_mat"], is_context)
    kc = vc = None
    if cache is not None:
        kc, vc = _kv_proj(cache[0], 0, cache[1], 0, QK_ROPE, cache[0].shape[0] // batch, None, None,
                          wts["wk_full"], wts["w_uv"], wts["e_mat"], False)
    att = _attention(q, kv[0], kv[1], kc, vc, batch, n_tok)
    s_f = None if states is None else _block_diag_state(states[0])
    s_b = None if states is None else _block_diag_state(states[1])
    y_f, new_f = _scan(proj_rw, batch, n_tok, wts["rwkv"], 0, s_f, is_context, None)
    rw, new_b = _scan(proj_rw, batch, n_tok, wts["rwkv"], 1, s_b, is_context, y_f)
    seq_tokens = n_tok if per_seq_mod else x2.shape[0]
    x1 = _out_proj(att, rw, x2, mod4, seq0, seq_tokens, wts["w_out_att"], wts["w_out_rw"], wts["ln1_g"], wts["ln1_b"])
    y = _ffn(x1, mod4, seq0, seq_tokens, wts["w_ffn_gate"], wts["w_ffn_up"], wts["w_ffn_down"],
             wts["ln2_g"], wts["ln2_b"])
    return y, kv, new_f, new_b


def kernel(x_prompt, x_sample, cache_ckv, cache_krope, state_wkv_fwd, state_wkv_bwd, c, c_ctx, w_mod, b_mod, w_in, q_norm_g, kv_norm_g, w_uq, w_uk, w_uv, tok_shift_mu, w0_fwd, w_up_fwd, a0_fwd, a_up_fwd, w0_bwd, w_up_bwd, a0_bwd, a_up_bwd, g_up, k_k, k_a, r_k, gn_g, gn_b, w_out, ln1_g, ln1_b, w_ffn_gate, w_ffn_up, w_ffn_down, ln2_g, ln2_b):
    batch, seq = x_prompt.shape[:2]
    dec_batch, dec_seq = x_sample.shape[:2]
    layer = 0

    mod_rows = 16
    cond = jnp.concatenate([c, c_ctx[None, :], jnp.zeros((mod_rows - dec_batch - 1, D_MODEL), F32)], 0)
    mod4 = _modulation(cond, w_mod[layer], b_mod[layer]).reshape(mod_rows, 6, 1, D_MODEL)

    wi = w_in[layer].astype(BF16)
    kr_w = wi[:, OFF_KR:OFF_RW]
    uq = w_uq[layer].reshape(Q_LORA, MLA_HEADS, QK_NOPE + QK_ROPE)
    uq_pad = jnp.zeros((Q_LORA, MLA_HEADS, QK_PAD - QK_NOPE - QK_ROPE), F32)
    wq_full = jnp.concatenate([uq, uq_pad], -1).reshape(Q_LORA, MLA_HEADS * QK_PAD)
    wq_sw = jnp.concatenate([_swap_halves(uq[..., QK_NOPE:]), uq_pad], -1).reshape(Q_LORA, MLA_HEADS * LANES)
    uk = w_uk[layer].reshape(KV_LORA, MLA_HEADS, QK_NOPE)
    wk_full = jnp.concatenate([uk, jnp.zeros((KV_LORA, MLA_HEADS, QK_PAD - QK_NOPE), F32)], -1)
    e_head = jnp.concatenate([jnp.zeros((QK_ROPE, QK_NOPE), F32), jnp.eye(QK_ROPE, dtype=F32),
                              jnp.zeros((QK_ROPE, QK_PAD - QK_NOPE - QK_ROPE), F32)], -1)
    lora_pad = jnp.zeros((DECAY_LORA, RWKV_WIDTH), F32)
    row = lambda vec: vec[layer].reshape(1, -1)
    head_of = jnp.arange(2 * LANES) // RWKV_HEAD
    tri_lo = jnp.tril(jnp.ones((CHUNK, CHUNK), F32))
    rwkv = {
        "mu": row(tok_shift_mu), "k_k": row(k_k), "k_a": row(k_a), "r_k": row(r_k),
        "gn_g": row(gn_g), "gn_b": row(gn_b), "g_up": g_up[layer].astype(BF16),
        "dirs": [
            {"w0": row(w0_fwd), "a0": row(a0_fwd),
             "w_up": jnp.concatenate([w_up_fwd[layer], lora_pad], 0).astype(BF16),
             "a_up": jnp.concatenate([lora_pad, a_up_fwd[layer]], 0).astype(BF16)},
            {"w0": row(w0_bwd), "a0": row(a0_bwd),
             "w_up": jnp.concatenate([w_up_bwd[layer], lora_pad], 0).astype(BF16),
             "a_up": jnp.concatenate([lora_pad, a_up_bwd[layer]], 0).astype(BF16)}],
        "ones_bd": (head_of[:, None] == head_of[None, :]).astype(BF16),
        "tri": [tri_lo.astype(BF16), tri_lo.T.astype(BF16)],
    }
    wts = {
        "w_mla": jnp.concatenate([wi[:, :OFF_RW], _swap_halves(kr_w)], -1),
        "w_rw": wi[:, OFF_RW:],
        "q_norm_g": q_norm_g[layer], "kv_norm_g": kv_norm_g[layer],
        "wq_full": wq_full.astype(BF16), "wq_sw": wq_sw.astype(BF16),
        "wk_full": wk_full.reshape(KV_LORA, MLA_HEADS * QK_PAD).astype(BF16),
        "w_uv": w_uv[layer].astype(BF16),
        "e_mat": jnp.tile(e_head, (1, MLA_HEADS)).astype(BF16),
        "rwkv": rwkv,
        "w_out_att": w_out[layer][:MLA_HEADS * V_HEAD].astype(BF16),
        "w_out_rw": w_out[layer][MLA_HEADS * V_HEAD:].astype(BF16),
        "ln1_g": ln1_g[layer], "ln1_b": ln1_b[layer],
        "w_ffn_gate": w_ffn_gate[layer].astype(BF16), "w_ffn_up": w_ffn_up[layer].astype(BF16),
        "w_ffn_down": w_ffn_down[layer].astype(BF16),
        "ln2_g": ln2_g[layer], "ln2_b": ln2_b[layer],
    }

    y_prompt, kv_ctx, s_f, s_b = _layer(
        x_prompt.reshape(batch * seq, D_MODEL), batch, seq, mod4, dec_batch, False, wts, None, None, None, True)
    past = cache_ckv.shape[2]
    cache = (cache_ckv[:, layer].reshape(dec_batch * past, KV_LORA),
             cache_krope[:, layer].reshape(dec_batch * past, QK_ROPE))
    y_sample, _, _, _ = _layer(
        x_sample.reshape(dec_batch * dec_seq, D_MODEL), dec_batch, dec_seq, mod4, 0, True, wts,
        _rope_tables(dec_seq), cache, (state_wkv_fwd[:, layer], state_wkv_bwd[:, layer]), False)

    return (y_prompt.reshape(batch, seq, D_MODEL),
            y_sample.reshape(dec_batch, dec_seq, D_MODEL),
            kv_ctx[2].reshape(batch, 1, seq, KV_LORA),
            kv_ctx[3].reshape(batch, 1, seq, QK_ROPE),
            _unblock_diag_state(s_f)[:, None],
            _unblock_diag_state(s_b)[:, None])
```

```python
import functools
import math

import jax
import jax.numpy as jnp
from jax import lax
from jax.experimental import pallas as pl
from jax.experimental.pallas import tpu as pltpu

F32 = jnp.float32
BF16 = jnp.bfloat16

D_MODEL = 2048
GRID_W = 64
MLA_HEADS = 8
QK_NOPE = 128
QK_ROPE = 64
V_HEAD = 128
Q_LORA = 512
KV_LORA = 256
ROPE_AXIS_FREQS = QK_ROPE // 4
ROPE_THETA = 10000.0
RWKV_HEADS = 16
RWKV_HEAD = 64
RWKV_WIDTH = RWKV_HEADS * RWKV_HEAD
DECAY_LORA = 64
ICLR_LORA = 64
GATE_LORA = 128
OFF_KV = Q_LORA
OFF_KR = OFF_KV + KV_LORA
OFF_RW = OFF_KR + QK_ROPE
RW_COLS = 3 * RWKV_WIDTH + DECAY_LORA + ICLR_LORA + GATE_LORA
D_FF = 5632
LN_EPS = 1e-5
RMS_EPS = 1e-6
GN_EPS = 64e-5
DEPTH = 1
ALPHA = (2.0 * DEPTH) ** 0.25

LANES = 128
QK_PAD = 2 * LANES
MLA_COLS = OFF_RW + QK_ROPE
CHUNK = 64
HEAD_PAIRS = RWKV_HEADS // 2
ATTN_TQ = 512
ATTN_HEADS = 4
IN_TM = 512
OUT_TM = 512
FFN_TM = 512
FFN_TF = 512
SCAN_TM = 512
SCAN_GROUP = 2
SCAN_PAIRS = 8
VMEM_LIMIT = 56 * 1024 * 1024


def _cp(sem):
    return pltpu.CompilerParams(dimension_semantics=sem, vmem_limit_bytes=VMEM_LIMIT)


def _dot(a, b):
    return jnp.dot(a, b, preferred_element_type=F32)


def _dot_nt(a, b):
    return lax.dot_general(a, b, (((1,), (1,)), ((), ())), preferred_element_type=F32)


def _dot_tn(a, b):
    return lax.dot_general(a, b, (((0,), (0,)), ((), ())), preferred_element_type=F32)


def _sigmoid(x):
    return 1.0 / (1.0 + jnp.exp(-x))


def _layer_norm(h, g, b):
    mu = jnp.mean(h, -1, keepdims=True)
    d = h - mu
    var = jnp.mean(d * d, -1, keepdims=True)
    return d * lax.rsqrt(var + LN_EPS) * g + b


def _sigmoid_tanh(x):
    return 0.5 * jnp.tanh(0.5 * x) + 0.5


def _split2(x):
    hi = x.astype(BF16)
    return hi, (x - hi.astype(F32)).astype(BF16)


def _dot2_right(x, m):
    hi, lo = _split2(x)
    return _dot(hi, m) + _dot(lo, m)


def _dot2_left(m, x):
    hi, lo = _split2(x)
    return _dot(m, hi) + _dot(m, lo)


def _mod_kernel(c_ref, w_ref, b_ref, o_ref):
    c = c_ref[...]
    s = (c * _sigmoid(c)).astype(BF16)
    o_ref[...] = _dot(s, w_ref[...].astype(BF16)) + b_ref[...]


def _modulation(cond, w_mod, b_mod):
    rows, tn = cond.shape[0], 1024
    n = w_mod.shape[1]
    return pl.pallas_call(
        _mod_kernel,
        grid=(n // tn,),
        in_specs=[pl.BlockSpec((rows, D_MODEL), lambda j: (0, 0)),
                  pl.BlockSpec((D_MODEL, tn), lambda j: (0, j)),
                  pl.BlockSpec((1, tn), lambda j: (0, j))],
        out_specs=pl.BlockSpec((rows, tn), lambda j: (0, j)),
        out_shape=jax.ShapeDtypeStruct((rows, n), F32),
        compiler_params=_cp(("arbitrary",)),
        name="modulation",
    )(cond, w_mod, b_mod.reshape(1, n))


def _mod_spec(which, seq_of):
    return pl.BlockSpec((None, None, 1, D_MODEL), lambda *g: (seq_of(*g), which, 0, 0))


def _in_kernel(x_ref, sh_ref, sc_ref, wm_ref, wr_ref, om_ref, or_ref):
    xm = (x_ref[...] * (1.0 + sc_ref[...]) + sh_ref[...]).astype(BF16)
    om_ref[...] = _dot(xm, wm_ref[...])
    or_ref[...] = _dot(xm, wr_ref[...])


def _in_proj(x2, mod4, seq0, n_seq_tokens, w_mla, w_rw):
    t, tm = x2.shape[0], IN_TM
    per_seq = n_seq_tokens // tm
    seq_of = (lambda i: seq0) if per_seq == 0 else (lambda i: seq0 + i // per_seq)
    resident = pl.Buffered(1)
    return pl.pallas_call(
        _in_kernel,
        grid=(t // tm,),
        in_specs=[pl.BlockSpec((tm, D_MODEL), lambda i: (i, 0)),
                  _mod_spec(0, seq_of), _mod_spec(1, seq_of),
                  pl.BlockSpec((D_MODEL, MLA_COLS), lambda i: (0, 0), pipeline_mode=resident),
                  pl.BlockSpec((D_MODEL, RW_COLS), lambda i: (0, 0), pipeline_mode=resident)],
        out_specs=[pl.BlockSpec((tm, MLA_COLS), lambda i: (i, 0)),
                   pl.BlockSpec((tm, RW_COLS), lambda i: (i, 0))],
        out_shape=[jax.ShapeDtypeStruct((t, MLA_COLS), F32),
                   jax.ShapeDtypeStruct((t, RW_COLS), F32)],
        compiler_params=_cp(("arbitrary",)),
        name="in_proj",
    )(x2, mod4, mod4, w_mla, w_rw)


def _q_kernel(*refs, rope):
    if rope:
        qd_ref, g_ref, w_ref, wsw_ref, c_ref, s_ref, o_ref = refs
    else:
        qd_ref, g_ref, w_ref, o_ref = refs
    x = qd_ref[...]
    qn = (x * lax.rsqrt(jnp.mean(x * x, -1, keepdims=True) + RMS_EPS) * g_ref[...]).astype(BF16)
    q = _dot(qn, w_ref[...])
    if rope:
        qs = _dot(qn, wsw_ref[...])
        c, s = c_ref[...], s_ref[...]
        parts = []
        for h in range(MLA_HEADS):
            lo = h * QK_PAD
            parts += [q[:, lo:lo + LANES],
                      q[:, lo + LANES:lo + QK_PAD] * c + qs[:, h * LANES:(h + 1) * LANES] * s]
        q = jnp.concatenate(parts, axis=1)
    o_ref[...] = q.astype(BF16)


def _q_proj(proj_mla, n_seq_tokens, q_norm_g, wq_full, wq_sw, rope_tabs):
    t, tm = proj_mla.shape[0], 256
    per_seq = n_seq_tokens // tm
    rope = rope_tabs is not None
    width = MLA_HEADS * QK_PAD
    const = lambda i: (0, 0)
    in_specs = [pl.BlockSpec((tm, Q_LORA), lambda i: (i, 0)),
                pl.BlockSpec((1, Q_LORA), const),
                pl.BlockSpec((Q_LORA, width), const)]
    args = [proj_mla, q_norm_g.reshape(1, Q_LORA), wq_full]
    if rope:
        in_specs += [pl.BlockSpec((Q_LORA, MLA_HEADS * LANES), const),
                     pl.BlockSpec((tm, LANES), lambda i: (i % per_seq, 0)),
                     pl.BlockSpec((tm, LANES), lambda i: (i % per_seq, 0))]
        args += [wq_sw, rope_tabs[0], rope_tabs[1]]
    return pl.pallas_call(
        functools.partial(_q_kernel, rope=rope),
        grid=(t // tm,),
        in_specs=in_specs,
        out_specs=pl.BlockSpec((tm, width), lambda i: (i, 0)),
        out_shape=jax.ShapeDtypeStruct((t, width), BF16),
        compiler_params=_cp(("arbitrary",)),
        name="q_proj",
    )(*args)


def _kv_kernel(*refs, norm, rope, emit_new):
    it = iter(refs)
    ckv_ref, kr_ref = next(it), next(it)
    g_ref = next(it) if norm else None
    cs_ref = next(it) if rope else None
    wk_ref, wv_ref, e_ref = next(it), next(it), next(it)
    k_out, v_out = next(it), next(it)
    x = ckv_ref[...]
    if norm:
        x = x * lax.rsqrt(jnp.mean(x * x, -1, keepdims=True) + RMS_EPS) * g_ref[...]
    xb = x.astype(BF16)
    kr = kr_ref[...]
    if rope:
        y = kr * cs_ref[...]
        kr = y + pltpu.roll(y, QK_ROPE, 1)
    kr = kr[:, :QK_ROPE]
    k_out[...] = (_dot(xb, wk_ref[...]) + _dot(kr.astype(BF16), e_ref[...])).T.astype(BF16)
    v_out[...] = _dot(xb, wv_ref[...]).astype(BF16)
    if emit_new:
        nc_out, nk_out = next(it), next(it)
        nc_out[...] = x
        nk_out[...] = kr


def _kv_proj(ckv_src, ckv_blk, kr_src, kr_blk, kr_w, n_seq_tokens, kv_norm_g, cs_tab, wk_full, wv, e_mat, emit_new):
    t, tm = ckv_src.shape[0], 256
    per_seq = n_seq_tokens // tm
    norm, rope = kv_norm_g is not None, cs_tab is not None
    kw, vw = MLA_HEADS * QK_PAD, MLA_HEADS * V_HEAD
    const = lambda i: (0, 0)
    in_specs = [pl.BlockSpec((tm, KV_LORA), lambda i: (i, ckv_blk)),
                pl.BlockSpec((tm, kr_w), lambda i: (i, kr_blk))]
    args = [ckv_src, kr_src]
    if norm:
        in_specs.append(pl.BlockSpec((1, KV_LORA), const))
        args.append(kv_norm_g.reshape(1, KV_LORA))
    if rope:
        in_specs.append(pl.BlockSpec((tm, LANES), lambda i: (i % per_seq, 0)))
        args.append(cs_tab)
    in_specs += [pl.BlockSpec((KV_LORA, kw), const), pl.BlockSpec((KV_LORA, vw), const),
                 pl.BlockSpec((QK_ROPE, kw), const)]
    args += [wk_full, wv, e_mat]
    out_specs = [pl.BlockSpec((kw, tm), lambda i: (i // per_seq, i % per_seq)),
                 pl.BlockSpec((tm, vw), lambda i: (i, 0))]
    out_shape = [jax.ShapeDtypeStruct((t // n_seq_tokens * kw, n_seq_tokens), BF16),
                 jax.ShapeDtypeStruct((t, vw), BF16)]
    if emit_new:
        out_specs += [pl.BlockSpec((tm, KV_LORA), lambda i: (i, 0)), pl.BlockSpec((tm, QK_ROPE), lambda i: (i, 0))]
        out_shape += [jax.ShapeDtypeStruct((t, KV_LORA), F32), jax.ShapeDtypeStruct((t, QK_ROPE), F32)]
    return pl.pallas_call(
        functools.partial(_kv_kernel, norm=norm, rope=rope, emit_new=emit_new),
        grid=(t // tm,),
        in_specs=in_specs, out_specs=out_specs, out_shape=out_shape,
        compiler_params=_cp(("arbitrary",)),
        name="kv_proj",
    )(*args)


def _attn_kernel(*refs, cache, heads, exp2_scale):
    if cache:
        q_ref, k_ref, v_ref, kc_ref, vc_ref, o_ref = refs
    else:
        q_ref, k_ref, v_ref, o_ref = refs
    out = [None] * heads

    def head_stages(h):
        qk = slice(h * QK_PAD, (h + 1) * QK_PAD)
        hv = slice(h * V_HEAD, (h + 1) * V_HEAD)
        q = q_ref[:, qk]
        s = _dot(q, k_ref[qk, :])
        sc = _dot(q, kc_ref[qk, :]) if cache else None
        yield
        m = jnp.max(s, -1, keepdims=True)
        if cache:
            m = jnp.maximum(m, jnp.max(sc, -1, keepdims=True))
        yield
        p = jnp.exp2((s - m) * exp2_scale).astype(BF16)
        if cache:
            pc = jnp.exp2((sc - m) * exp2_scale).astype(BF16)
        yield
        v = v_ref[:, hv]
        acc = _dot(p, jnp.concatenate([v, jnp.ones_like(v)], axis=1))
        if cache:
            vc = vc_ref[:, hv]
            acc = acc + _dot(pc, jnp.concatenate([vc, jnp.ones_like(vc)], axis=1))
        out[h] = acc[:, :V_HEAD] * (1.0 / acc[:, V_HEAD:])
        yield

    live, waiting = [], [head_stages(h) for h in range(heads)]
    while live or waiting:
        if waiting:
            live.append(waiting.pop(0))
        live = [g for g in live if next(g, StopIteration) is not StopIteration]
    o_ref[...] = jnp.concatenate(out, axis=1).astype(BF16)


def _attention(q, kt, v, kct, vc, batch, n_tok):
    tq, heads = min(ATTN_TQ, n_tok), ATTN_HEADS
    n_q = n_tok // tq
    groups = MLA_HEADS // heads
    cache = kct is not None
    in_specs = [pl.BlockSpec((tq, heads * QK_PAD), lambda b, h, i: (b * n_q + i, h)),
                pl.BlockSpec((heads * QK_PAD, n_tok), lambda b, h, i: (b * groups + h, 0)),
                pl.BlockSpec((n_tok, heads * V_HEAD), lambda b, h, i: (b, h))]
    args = [q, kt, v]
    if cache:
        n_c = vc.shape[0] // batch
        in_specs += [pl.BlockSpec((heads * QK_PAD, n_c), lambda b, h, i: (b * groups + h, 0)),
                     pl.BlockSpec((n_c, heads * V_HEAD), lambda b, h, i: (b, h))]
        args += [kct, vc]
    scale = 1.0 / math.sqrt(QK_NOPE + QK_ROPE)
    return pl.pallas_call(
        functools.partial(_attn_kernel, cache=cache, heads=heads, exp2_scale=scale * math.log2(math.e)),
        grid=(batch, groups, n_q),
        in_specs=in_specs,
        out_specs=pl.BlockSpec((tq, heads * V_HEAD), lambda b, h, i: (b * n_q + i, h)),
        out_shape=jax.ShapeDtypeStruct((batch * n_tok, MLA_HEADS * V_HEAD), BF16),
        compiler_params=_cp(("arbitrary", "arbitrary", "arbitrary")),
        name="attention",
    )(*args)


def _pair_blockdiag(y, lane_lo):
    zero = jnp.zeros_like(y)
    return jnp.concatenate([jnp.where(lane_lo, y, zero), jnp.where(lane_lo, zero, y)], axis=0)


def _scan_kernel(*refs, tm, npair, reverse, zero_init, emit_state, finalize):
    it = iter(refs)
    blocks = {}
    cols = ("r", "k", "v", "wa") + (("gd",) if finalize else ())
    for name in cols:
        blocks[name] = (next(it), next(it), next(it), next(it))
    kk_ref, ka_ref, w0_ref, wup_ref, a0_ref, aup_ref = (next(it) for _ in range(6))
    if finalize:
        rk_ref, a0o_ref, aupo_ref, gup_ref, gng_ref, gnb_ref, yo_ref = (next(it) for _ in range(7))
    ones_ref, tri_ref = next(it), next(it)
    s0_ref = None if zero_init else next(it)
    y_ref = next(it)
    sout_ref = next(it) if emit_state else None
    s_scr = next(it)

    t_idx = pl.program_id(2)
    n_t = pl.num_programs(2)
    tile = (n_t - 1 - t_idx) if reverse else t_idx
    first_tok = tile == 0
    last_tok = tile == n_t - 1

    @pl.when(t_idx == 0)
    def _():
        if zero_init:
            s_scr[...] = jnp.zeros_like(s_scr)
        else:
            s_scr[...] = s0_ref[...]

    def shifted(name):
        main, prv, nxt, mu = blocks[name]
        u = main[...]
        row = lax.broadcasted_iota(jnp.int32, u.shape, 0)
        p_row = jnp.where(first_tok, 0.0, prv[7:8, :])
        n_row = jnp.where(last_tok, 0.0, nxt[0:1, :])
        up = jnp.where(row == 0, p_row, pltpu.roll(u, 1, 0))
        un = jnp.where(row == tm - 1, n_row, pltpu.roll(u, tm - 1, 0))
        return u + mu[...] * (0.5 * (up + un) - u)

    ones_bd = ones_ref[...]

    seg_w = ones_bd.shape[0]

    def seg(x):
        return jnp.concatenate(
            [_dot2_right(x[:, j * seg_w:(j + 1) * seg_w], ones_bd) for j in range(x.shape[1] // seg_w)], axis=1)

    r, k, v, wa = shifted("r"), shifted("k"), shifted("v"), shifted("wa")
    wab = wa.astype(BF16)
    kk = k * kk_ref[...]
    kk = kk * lax.rsqrt(jnp.maximum(seg(kk * kk), 1e-24))
    ld = _sigmoid_tanh(w0_ref[...] + _dot(jnp.tanh(wa).astype(BF16), wup_ref[...])) * (-math.exp(-0.5))
    a = _sigmoid_tanh(a0_ref[...] + _dot(wab, aup_ref[...]))
    kd = k * (1.0 + (a - 1.0) * ka_ref[...])
    b = kk * a

    c_row = lax.broadcasted_iota(jnp.int32, (CHUNK, LANES), 0)
    c_lane = lax.broadcasted_iota(jnp.int32, (CHUNK, LANES), 1)
    c_src = c_lane & (CHUNK - 1)
    lane_lo = c_lane < CHUNK
    strict = (c_src > c_row) if reverse else (c_src < c_row)
    incl = (c_src >= c_row) if reverse else (c_src <= c_row)
    eye = (c_src == c_row).astype(F32)
    bd_row = lax.broadcasted_iota(jnp.int32, (LANES, LANES), 0)
    bd_lane = lax.broadcasted_iota(jnp.int32, (LANES, LANES), 1)
    bd_mask = jnp.right_shift(bd_row, 6) == jnp.right_shift(bd_lane, 6)
    tri = tri_ref[...]

    def nn(p, y):
        return _dot(p.astype(BF16), _pair_blockdiag(y.astype(BF16), lane_lo))

    n_chunks = tm // CHUNK
    order = list(range(n_chunks - 1, -1, -1) if reverse else range(n_chunks))
    cums = [_dot2_left(tri, ld[c * CHUNK:(c + 1) * CHUNK]) for c in range(n_chunks)]
    pairs = range(npair)
    strict2 = jnp.concatenate([strict, strict], axis=1)
    incl2 = jnp.concatenate([incl, incl], axis=1)
    st = {}

    def independent_stage(c):
        sl = slice(c * CHUNK, (c + 1) * CHUNK)
        for j in pairs:
            ln = slice(j * LANES, (j + 1) * LANES)
            r_c, v_c, kk_c, kd_c, b_c, ld_c = r[sl, ln], v[sl, ln], kk[sl, ln], kd[sl, ln], b[sl, ln], ld[sl, ln]
            cum = cums[c][:, ln]
            tot = cum[0:1] if reverse else cum[CHUNK - 1:CHUNK]
            e_neg = jnp.exp(-cum)
            e_tot = jnp.exp(tot - cum)
            a_hat = -kk_c * jnp.exp(cum - ld_c)
            r_hat = r_c * jnp.exp(cum)
            ar = jnp.concatenate([a_hat, r_hat], axis=0).astype(BF16)
            bk_hat = jnp.concatenate([_pair_blockdiag((b_c * e_neg).astype(BF16), lane_lo),
                                      _pair_blockdiag((kd_c * e_neg).astype(BF16), lane_lo)], axis=0)
            g = _dot_nt(ar, bk_hat)
            gs = jnp.where(strict2, g[:CHUNK], 0.0)
            lmat = gs[:, :LANES]
            st[j, c] = dict(
                ar=ar, v=v_c.astype(BF16), v_bd=_pair_blockdiag(v_c.astype(BF16), lane_lo),
                p=eye + lmat, q=lmat, akm=gs[:, LANES:],
                rbk=jnp.where(incl2, g[CHUNK:], 0.0).astype(BF16),
                bk=jnp.concatenate([b_c * e_tot, kd_c * e_tot], axis=0).astype(BF16),
                w=jnp.exp(tot))
        yield
        for j in pairs:
            st[j, c]["q"] = nn(st[j, c]["q"], st[j, c]["q"])
        yield
        for _ in range(int(math.log2(CHUNK)) - 2):
            for j in pairs:
                d = st[j, c]
                pq = nn(jnp.concatenate([d["p"], d["q"]], axis=0), d["q"])
                d["p"], d["q"] = d["p"] + pq[:CHUNK], pq[CHUNK:]
            yield
        for j in pairs:
            d = st[j, c]
            d["p"] = d["p"] + nn(d["p"], d["q"])
            d["akv"] = _dot(d["akm"].astype(BF16), d["v_bd"])
        yield

    s_cur = [s_scr[j] for j in pairs]
    y_cells = {}

    def dependent_stage(c):
        xs = [_dot_nt(st[j, c]["ar"], s_cur[j].astype(BF16)) for j in pairs]
        yield
        u = [nn(st[j, c]["p"], xs[j][:CHUNK] + st[j, c]["akv"]) for j in pairs]
        yield
        for j in pairs:
            d = st[j, c]
            y_cells[j, c] = xs[j][CHUNK:] + _dot(
                d["rbk"],
                jnp.concatenate([_pair_blockdiag(u[j].astype(BF16), lane_lo), d["v_bd"]], axis=0))
            upd = _dot_tn(jnp.concatenate([u[j].astype(BF16), d["v"]], axis=0), d["bk"])
            s_cur[j] = s_cur[j] * d["w"] + jnp.where(bd_mask, upd, 0.0)
        yield

    def run_side_by_side(*stages):
        live = list(stages)
        while live:
            for g in list(live):
                if next(g, StopIteration) is StopIteration:
                    live.remove(g)

    def in_sequence(stages):
        for g in stages:
            yield from g

    gd = shifted("gd") if finalize else None

    def output_stage(chunks):
        rows = slice(min(chunks) * CHUNK, (max(chunks) + 1) * CHUNK)
        y = jnp.concatenate(
            [jnp.concatenate([y_cells[j, c] for c in sorted(chunks)], axis=0) for j in pairs], axis=1)
        if not finalize:
            y_ref[rows, :] = y
            return
        a_o = _sigmoid_tanh(a0o_ref[...] + _dot(wab[rows], aupo_ref[...]))
        kd_o = k[rows] * (1.0 + (a_o - 1.0) * ka_ref[...])
        bonus = seg(r[rows] * rk_ref[...] * (kd[rows] + kd_o))
        gate = _dot(_sigmoid_tanh(gd[rows]).astype(BF16), gup_ref[...])
        yield
        yt = y + yo_ref[rows, :]
        inv_n = 1.0 / RWKV_HEAD
        dev = yt - seg(yt) * inv_n
        yield
        var = seg(dev * dev) * inv_n
        yn = dev * lax.rsqrt(var + GN_EPS) * gng_ref[...] + gnb_ref[...]
        y_ref[rows, :] = ((yn + bonus * v[rows]) * gate).astype(BF16)
        yield

    groups = [order[i:i + SCAN_GROUP] for i in range(0, n_chunks, SCAN_GROUP)]
    early = len(groups) // 2 if len(groups) >= 4 else 0
    for gi in range(len(groups) + 1):
        stages = [independent_stage(c) for c in groups[gi]] if gi < len(groups) else []
        if gi > 0:
            stages.append(in_sequence([dependent_stage(c) for c in groups[gi - 1]]))
        if early and gi == early + 1:
            stages.append(output_stage([c for g in groups[:early] for c in g]))
        run_side_by_side(*stages)
    run_side_by_side(output_stage([c for g in groups[early:] for c in g]))
    s_new = s_cur
    for j in range(npair):
        s_scr[j] = s_new[j]

    if emit_state:
        @pl.when(t_idx == n_t - 1)
        def _():
            for j in range(npair):
                sout_ref[j] = s_new[j]


def _scan(proj_rw, batch, n_tok, prm, direction, s0_bd, emit_state, y_other):
    reverse = direction == 1
    finalize = y_other is not None
    zero_init = s0_bd is None
    t = batch * n_tok
    tm = min(SCAN_TM, n_tok)
    npair = SCAN_PAIRS
    width = npair * LANES
    groups = HEAD_PAIRS // npair
    n_t = n_tok // tm
    row8 = tm // 8
    n_row8 = t // 8

    def tile_of(ti):
        return (n_t - 1 - ti) if reverse else ti

    def token_specs(w, cb):
        return [pl.BlockSpec((tm, w), lambda bb, p, ti: (bb * n_t + tile_of(ti), cb(p))),
                pl.BlockSpec((8, w), lambda bb, p, ti: (jnp.maximum((bb * n_t + tile_of(ti)) * row8 - 1, 0), cb(p))),
                pl.BlockSpec((8, w), lambda bb, p, ti: (jnp.minimum((bb * n_t + tile_of(ti) + 1) * row8, n_row8 - 1),
                                                        cb(p))),
                pl.BlockSpec((1, w), lambda bb, p, ti: (0, cb(p)))]

    rowvec = lambda: pl.BlockSpec((1, width), lambda bb, p, ti: (0, p))
    lora = lambda: pl.BlockSpec((LANES, width), lambda bb, p, ti: (0, p))
    const = lambda shape: pl.BlockSpec(shape, lambda bb, p, ti: (0, 0))
    state = lambda: pl.BlockSpec((None, npair, LANES, LANES), lambda bb, p, ti: (bb, p, 0, 0))
    y_spec = lambda: pl.BlockSpec((tm, width), lambda bb, p, ti: (bb * n_t + tile_of(ti), p))

    lora_blk = 3 * RWKV_WIDTH // LANES
    in_specs, args = [], []
    for name, w, cb in (("r", width, lambda p: p), ("k", width, lambda p: groups + p),
                        ("v", width, lambda p: 2 * groups + p), ("wa", LANES, lambda p: lora_blk),
                        ("gd", LANES, lambda p: lora_blk + 1)):
        if name == "gd" and not finalize:
            continue
        in_specs += token_specs(w, cb)
        args += [proj_rw, proj_rw, proj_rw, prm["mu"]]
    d, o = prm["dirs"][direction], prm["dirs"][1 - direction]
    in_specs += [rowvec(), rowvec(), rowvec(), lora(), rowvec(), lora()]
    args += [prm["k_k"], prm["k_a"], d["w0"], d["w_up"], d["a0"], d["a_up"]]
    if finalize:
        in_specs += [rowvec(), rowvec(), lora(), lora(), rowvec(), rowvec(), y_spec()]
        args += [prm["r_k"], o["a0"], o["a_up"], prm["g_up"], prm["gn_g"], prm["gn_b"], y_other]
    in_specs += [const(prm["ones_bd"].shape), const((CHUNK, CHUNK))]
    args += [prm["ones_bd"], prm["tri"][direction]]
    if not zero_init:
        in_specs.append(state())
        args.append(s0_bd)
    out_specs = [y_spec()]
    out_shape = [jax.ShapeDtypeStruct((t, RWKV_WIDTH), BF16 if finalize else F32)]
    if emit_state:
        out_specs.append(state())
        out_shape.append(jax.ShapeDtypeStruct((batch, HEAD_PAIRS, LANES, LANES), F32))
    out = pl.pallas_call(
        functools.partial(_scan_kernel, tm=tm, npair=npair, reverse=reverse, zero_init=zero_init,
                          emit_state=emit_state, finalize=finalize),
        grid=(batch, groups, n_t),
        in_specs=in_specs, out_specs=out_specs, out_shape=out_shape,
        scratch_shapes=[pltpu.VMEM((npair, LANES, LANES), F32)],
        compiler_params=_cp(("arbitrary", "arbitrary", "arbitrary")),
        name="rwkv_scan_bwd" if reverse else "rwkv_scan_fwd",
    )(*args)
    return out if emit_state else (out[0], None)


def _out_kernel(att_ref, rw_ref, x_ref, g_ref, wa_ref, wr_ref, lg_ref, lb_ref, o_ref):
    half = o_ref.shape[0] // 2
    for rows in (slice(0, half), slice(half, 2 * half)):
        f = _dot(att_ref[rows, :], wa_ref[...]) + _dot(rw_ref[rows, :], wr_ref[...])
        o_ref[rows, :] = _layer_norm(ALPHA * x_ref[rows, :] + g_ref[...] * f, lg_ref[...], lb_ref[...])


def _out_proj(att, rw, x2, mod4, seq0, n_seq_tokens, w_att, w_rw, ln_g, ln_b):
    t, tm = x2.shape[0], min(OUT_TM, n_seq_tokens)
    per_seq = n_seq_tokens // tm
    seq_of = lambda i: seq0 + i // per_seq
    half = D_MODEL // 2
    const = lambda i: (0, 0)
    return pl.pallas_call(
        _out_kernel,
        grid=(t // tm,),
        in_specs=[pl.BlockSpec((tm, half), lambda i: (i, 0)), pl.BlockSpec((tm, half), lambda i: (i, 0)),
                  pl.BlockSpec((tm, D_MODEL), lambda i: (i, 0)), _mod_spec(2, seq_of),
                  pl.BlockSpec((half, D_MODEL), const), pl.BlockSpec((half, D_MODEL), const),
                  pl.BlockSpec((1, D_MODEL), const), pl.BlockSpec((1, D_MODEL), const)],
        out_specs=pl.BlockSpec((tm, D_MODEL), lambda i: (i, 0)),
        out_shape=jax.ShapeDtypeStruct((t, D_MODEL), F32),
        compiler_params=_cp(("arbitrary",)),
        name="out_proj",
    )(att, rw, x2, mod4, w_att, w_rw, ln_g.reshape(1, D_MODEL), ln_b.reshape(1, D_MODEL))


def _ffn_kernel(x_ref, sh_ref, sc_ref, g_ref, wg_ref, wu_ref, wd_ref, lg_ref, lb_ref, o_ref, h_scr, acc_scr):
    kf = pl.program_id(1)

    @pl.when(kf == 0)
    def _():
        h_scr[...] = (x_ref[...] * (1.0 + sc_ref[...]) + sh_ref[...]).astype(BF16)
        acc_scr[...] = jnp.zeros_like(acc_scr)

    h = h_scr[...]
    gate = _dot(h, wg_ref[...])
    act = (gate * _sigmoid(gate) * _dot(h, wu_ref[...])).astype(BF16)
    acc_scr[...] += _dot(act, wd_ref[...])

    @pl.when(kf == pl.num_programs(1) - 1)
    def _():
        o_ref[...] = _layer_norm(ALPHA * x_ref[...] + g_ref[...] * acc_scr[...], lg_ref[...], lb_ref[...])


def _ffn(x1, mod4, seq0, n_seq_tokens, w_gate, w_up, w_down, ln_g, ln_b):
    t, tf = x1.shape[0], FFN_TF
    tm = min(FFN_TM, n_seq_tokens)
    per_seq = n_seq_tokens // tm
    seq_of = lambda i, kf: seq0 + i // per_seq
    const = lambda i, kf: (0, 0)
    return pl.pallas_call(
        _ffn_kernel,
        grid=(t // tm, D_FF // tf),
        in_specs=[pl.BlockSpec((tm, D_MODEL), lambda i, kf: (i, 0)),
                  _mod_spec(3, seq_of), _mod_spec(4, seq_of), _mod_spec(5, seq_of),
                  pl.BlockSpec((D_MODEL, tf), lambda i, kf: (0, kf)),
                  pl.BlockSpec((D_MODEL, tf), lambda i, kf: (0, kf)),
                  pl.BlockSpec((tf, D_MODEL), lambda i, kf: (kf, 0)),
                  pl.BlockSpec((1, D_MODEL), const), pl.BlockSpec((1, D_MODEL), const)],
        out_specs=pl.BlockSpec((tm, D_MODEL), lambda i, kf: (i, 0)),
        out_shape=jax.ShapeDtypeStruct((t, D_MODEL), F32),
        scratch_shapes=[pltpu.VMEM((tm, D_MODEL), BF16), pltpu.VMEM((tm, D_MODEL), F32)],
        compiler_params=_cp(("arbitrary", "arbitrary")),
        name="ffn",
    )(x1, mod4, mod4, mod4, w_gate, w_up, w_down, ln_g.reshape(1, D_MODEL), ln_b.reshape(1, D_MODEL))


def _rope_tables(n_tokens):
    rows = n_tokens // GRID_W
    row = jnp.repeat(jnp.arange(rows), GRID_W).astype(F32)
    col = jnp.tile(jnp.arange(GRID_W), rows).astype(F32)
    freqs = ROPE_THETA ** (-jnp.arange(ROPE_AXIS_FREQS, dtype=F32) / ROPE_AXIS_FREQS)
    ang = jnp.concatenate([row[:, None] * freqs, col[:, None] * freqs], -1)
    cos, sin = jnp.cos(ang), jnp.sin(ang)
    cos2 = jnp.concatenate([cos, cos], -1)
    sin2 = jnp.concatenate([-sin, sin], -1)
    pad = jnp.zeros((n_tokens, LANES - QK_ROPE), F32)
    q_cos = jnp.concatenate([cos2, pad], -1)
    q_sin = jnp.concatenate([sin2, pad], -1)
    k_tab = jnp.concatenate([cos2, sin2], -1)
    return q_cos, q_sin, k_tab


def _swap_halves(w):
    half = w.shape[-1] // 2
    return jnp.concatenate([w[..., half:], w[..., :half]], -1)


def _block_diag_state(s):
    bsz = s.shape[0]
    s = s.reshape(bsz, HEAD_PAIRS, 2, RWKV_HEAD, RWKV_HEAD)
    z = jnp.zeros_like(s[:, :, 0])
    top = jnp.concatenate([s[:, :, 0], z], -1)
    bot = jnp.concatenate([z, s[:, :, 1]], -1)
    return jnp.concatenate([top, bot], -2)


def _unblock_diag_state(s_bd):
    bsz = s_bd.shape[0]
    s = jnp.stack([s_bd[:, :, :RWKV_HEAD, :RWKV_HEAD], s_bd[:, :, RWKV_HEAD:, RWKV_HEAD:]], 2)
    return s.reshape(bsz, RWKV_HEADS, RWKV_HEAD, RWKV_HEAD)


def _layer(x2, batch, n_tok, mod4, seq0, per_seq_mod, wts, rope_tabs, cache, states, is_context):
    mod_tokens = n_tok if per_seq_mod else 0
    proj_mla, proj_rw = _in_proj(x2, mod4, seq0, mod_tokens, wts["w_mla"], wts["w_rw"])
    q = _q_proj(proj_mla, n_tok, wts["q_norm_g"], wts["wq_full"], wts["wq_sw"],
                None if rope_tabs is None else rope_tabs[:2])
    kv = _kv_proj(proj_mla, OFF_KV // KV_LORA, proj_mla, OFF_KR // LANES, LANES, n_tok, wts["kv_norm_g"],
                  None if rope_tabs is None else rope_tabs[2], wts["wk_full"], wts["w_uv"], wts["e_mat"], is_context)
    kc = vc = None
    if cache is not None:
        kc, vc = _kv_proj(cache[0], 0, cache[1], 0, QK_ROPE, cache[0].shape[0] // batch, None, None,
                          wts["wk_full"], wts["w_uv"], wts["e_mat"], False)
    att = _attention(q, kv[0], kv[1], kc, vc, batch, n_tok)
    s_f = None if states is None else _block_diag_state(states[0])
    s_b = None if states is None else _block_diag_state(states[1])
    y_f, new_f = _scan(proj_rw, batch, n_tok, wts["rwkv"], 0, s_f, is_context, None)
    rw, new_b = _scan(proj_rw, batch, n_tok, wts["rwkv"], 1, s_b, is_context, y_f)
    seq_tokens = n_tok if per_seq_mod else x2.shape[0]
    x1 = _out_proj(att, rw, x2, mod4, seq0, seq_tokens, wts["w_out_att"], wts["w_out_rw"], wts["ln1_g"], wts["ln1_b"])
    y = _ffn(x1, mod4, seq0, seq_tokens, wts["w_ffn_gate"], wts["w_ffn_up"], wts["w_ffn_down"],
             wts["ln2_g"], wts["ln2_b"])
    return y, kv, new_f, new_b


def kernel(x_prompt, x_sample, cache_ckv, cache_krope, state_wkv_fwd, state_wkv_bwd, c, c_ctx, w_mod, b_mod, w_in, q_norm_g, kv_norm_g, w_uq, w_uk, w_uv, tok_shift_mu, w0_fwd, w_up_fwd, a0_fwd, a_up_fwd, w0_bwd, w_up_bwd, a0_bwd, a_up_bwd, g_up, k_k, k_a, r_k, gn_g, gn_b, w_out, ln1_g, ln1_b, w_ffn_gate, w_ffn_up, w_ffn_down, ln2_g, ln2_b):
    batch, seq = x_prompt.shape[:2]
    dec_batch, dec_seq = x_sample.shape[:2]
    layer = 0

    mod_rows = 16
    cond = jnp.concatenate([c, c_ctx[None, :], jnp.zeros((mod_rows - dec_batch - 1, D_MODEL), F32)], 0)
    mod4 = _modulation(cond, w_mod[layer], b_mod[layer]).reshape(mod_rows, 6, 1, D_MODEL)

    wi = w_in[layer].astype(BF16)
    kr_w = wi[:, OFF_KR:OFF_RW]
    uq = w_uq[layer].reshape(Q_LORA, MLA_HEADS, QK_NOPE + QK_ROPE)
    uq_pad = jnp.zeros((Q_LORA, MLA_HEADS, QK_PAD - QK_NOPE - QK_ROPE), F32)
    wq_full = jnp.concatenate([uq, uq_pad], -1).reshape(Q_LORA, MLA_HEADS * QK_PAD)
    wq_sw = jnp.concatenate([_swap_halves(uq[..., QK_NOPE:]), uq_pad], -1).reshape(Q_LORA, MLA_HEADS * LANES)
    uk = w_uk[layer].reshape(KV_LORA, MLA_HEADS, QK_NOPE)
    wk_full = jnp.concatenate([uk, jnp.zeros((KV_LORA, MLA_HEADS, QK_PAD - QK_NOPE), F32)], -1)
    e_head = jnp.concatenate([jnp.zeros((QK_ROPE, QK_NOPE), F32), jnp.eye(QK_ROPE, dtype=F32),
                              jnp.zeros((QK_ROPE, QK_PAD - QK_NOPE - QK_ROPE), F32)], -1)
    lora_pad = jnp.zeros((DECAY_LORA, RWKV_WIDTH), F32)
    row = lambda vec: vec[layer].reshape(1, -1)
    head_of = jnp.arange(2 * LANES) // RWKV_HEAD
    tri_lo = jnp.tril(jnp.ones((CHUNK, CHUNK), F32))
    rwkv = {
        "mu": row(tok_shift_mu), "k_k": row(k_k), "k_a": row(k_a), "r_k": row(r_k),
        "gn_g": row(gn_g), "gn_b": row(gn_b), "g_up": g_up[layer].astype(BF16),
        "dirs": [
            {"w0": row(w0_fwd), "a0": row(a0_fwd),
             "w_up": jnp.concatenate([w_up_fwd[layer], lora_pad], 0).astype(BF16),
             "a_up": jnp.concatenate([lora_pad, a_up_fwd[layer]], 0).astype(BF16)},
            {"w0": row(w0_bwd), "a0": row(a0_bwd),
             "w_up": jnp.concatenate([w_up_bwd[layer], lora_pad], 0).astype(BF16),
             "a_up": jnp.concatenate([lora_pad, a_up_bwd[layer]], 0).astype(BF16)}],
        "ones_bd": (head_of[:, None] == head_of[None, :]).astype(BF16),
        "tri": [tri_lo.astype(BF16), tri_lo.T.astype(BF16)],
    }
    wts = {
        "w_mla": jnp.concatenate([wi[:, :OFF_RW], _swap_halves(kr_w)], -1),
        "w_rw": wi[:, OFF_RW:],
        "q_norm_g": q_norm_g[layer], "kv_norm_g": kv_norm_g[layer],
        "wq_full": wq_full.astype(BF16), "wq_sw": wq_sw.astype(BF16),
        "wk_full": wk_full.reshape(KV_LORA, MLA_HEADS * QK_PAD).astype(BF16),
        "w_uv": w_uv[layer].astype(BF16),
        "e_mat": jnp.tile(e_head, (1, MLA_HEADS)).astype(BF16),
        "rwkv": rwkv,
        "w_out_att": w_out[layer][:MLA_HEADS * V_HEAD].astype(BF16),
        "w_out_rw": w_out[layer][MLA_HEADS * V_HEAD:].astype(BF16),
        "ln1_g": ln1_g[layer], "ln1_b": ln1_b[layer],
        "w_ffn_gate": w_ffn_gate[layer].astype(BF16), "w_ffn_up": w_ffn_up[layer].astype(BF16),
        "w_ffn_down": w_ffn_down[layer].astype(BF16),
        "ln2_g": ln2_g[layer], "ln2_b": ln2_b[layer],
    }

    y_prompt, kv_ctx, s_f, s_b = _layer(
        x_prompt.reshape(batch * seq, D_MODEL), batch, seq, mod4, dec_batch, False, wts, None, None, None, True)
    past = cache_ckv.shape[2]
    cache = (cache_ckv[:, layer].reshape(dec_batch * past, KV_LORA),
             cache_krope[:, layer].reshape(dec_batch * past, QK_ROPE))
    y_sample, _, _, _ = _layer(
        x_sample.reshape(dec_batch * dec_seq, D_MODEL), dec_batch, dec_seq, mod4, 0, True, wts,
        _rope_tables(dec_seq), cache, (state_wkv_fwd[:, layer], state_wkv_bwd[:, layer]), False)

    return (y_prompt.reshape(batch, seq, D_MODEL),
            y_sample.reshape(dec_batch, dec_seq, D_MODEL),
            kv_ctx[2].reshape(batch, 1, seq, KV_LORA),
            kv_ctx[3].reshape(batch, 1, seq, QK_ROPE),
            _unblock_diag_state(s_f)[:, None],
            _unblock_diag_state(s_b)[:, None])
```

```python
import functools
import math

import jax
import jax.numpy as jnp
from jax import lax
from jax.experimental import pallas as pl
from jax.experimental.pallas import tpu as pltpu

F32 = jnp.float32
BF16 = jnp.bfloat16

D_MODEL = 2048
GRID_W = 64
MLA_HEADS = 8
QK_NOPE = 128
QK_ROPE = 64
V_HEAD = 128
Q_LORA = 512
KV_LORA = 256
ROPE_AXIS_FREQS = QK_ROPE // 4
ROPE_THETA = 10000.0
RWKV_HEADS = 16
RWKV_HEAD = 64
RWKV_WIDTH = RWKV_HEADS * RWKV_HEAD
DECAY_LORA = 64
ICLR_LORA = 64
GATE_LORA = 128
OFF_KV = Q_LORA
OFF_KR = OFF_KV + KV_LORA
OFF_RW = OFF_KR + QK_ROPE
RW_COLS = 3 * RWKV_WIDTH + DECAY_LORA + ICLR_LORA + GATE_LORA
D_FF = 5632
LN_EPS = 1e-5
RMS_EPS = 1e-6
GN_EPS = 64e-5
DEPTH = 1
ALPHA = (2.0 * DEPTH) ** 0.25

LANES = 128
QK_PAD = 2 * LANES
MLA_COLS = OFF_RW + QK_ROPE
CHUNK = 64
HEAD_PAIRS = RWKV_HEADS // 2
ATTN_TQ = 512
ATTN_HEADS = 4
IN_TM = 512
OUT_TM = 512
FFN_TM = 512
FFN_TF = 512
SCAN_TM = 512
SCAN_GROUP = 2
SCAN_PAIRS = 8
VMEM_LIMIT = 56 * 1024 * 1024


def _cp(sem):
    return pltpu.CompilerParams(dimension_semantics=sem, vmem_limit_bytes=VMEM_LIMIT)


def _dot(a, b):
    return jnp.dot(a, b, preferred_element_type=F32)


def _dot_nt(a, b):
    return lax.dot_general(a, b, (((1,), (1,)), ((), ())), preferred_element_type=F32)


def _dot_tn(a, b):
    return lax.dot_general(a, b, (((0,), (0,)), ((), ())), preferred_element_type=F32)


def _sigmoid(x):
    return 1.0 / (1.0 + jnp.exp(-x))


def _layer_norm(h, g, b):
    mu = jnp.mean(h, -1, keepdims=True)
    d = h - mu
    var = jnp.mean(d * d, -1, keepdims=True)
    return d * lax.rsqrt(var + LN_EPS) * g + b


def _sigmoid_tanh(x):
    return 0.5 * jnp.tanh(0.5 * x) + 0.5


def _split2(x):
    hi = x.astype(BF16)
    return hi, (x - hi.astype(F32)).astype(BF16)


def _dot2_right(x, m):
    hi, lo = _split2(x)
    return _dot(hi, m) + _dot(lo, m)


def _dot2_left(m, x):
    hi, lo = _split2(x)
    return _dot(m, hi) + _dot(m, lo)


def _mod_kernel(c_ref, w_ref, b_ref, o_ref):
    c = c_ref[...]
    s = (c * _sigmoid(c)).astype(BF16)
    o_ref[...] = _dot(s, w_ref[...].astype(BF16)) + b_ref[...]


def _modulation(cond, w_mod, b_mod):
    rows, tn = cond.shape[0], 1024
    n = w_mod.shape[1]
    return pl.pallas_call(
        _mod_kernel,
        grid=(n // tn,),
        in_specs=[pl.BlockSpec((rows, D_MODEL), lambda j: (0, 0)),
                  pl.BlockSpec((D_MODEL, tn), lambda j: (0, j)),
                  pl.BlockSpec((1, tn), lambda j: (0, j))],
        out_specs=pl.BlockSpec((rows, tn), lambda j: (0, j)),
        out_shape=jax.ShapeDtypeStruct((rows, n), F32),
        compiler_params=_cp(("arbitrary",)),
        name="modulation",
    )(cond, w_mod, b_mod.reshape(1, n))


def _mod_spec(which, seq_of):
    return pl.BlockSpec((None, None, 1, D_MODEL), lambda *g: (seq_of(*g), which, 0, 0))


def _in_kernel(x_ref, sh_ref, sc_ref, wm_ref, wr_ref, om_ref, or_ref):
    xm = (x_ref[...] * (1.0 + sc_ref[...]) + sh_ref[...]).astype(BF16)
    om_ref[...] = _dot(xm, wm_ref[...])
    or_ref[...] = _dot(xm, wr_ref[...])


def _in_proj(x2, mod4, seq0, n_seq_tokens, w_mla, w_rw):
    t, tm = x2.shape[0], IN_TM
    per_seq = n_seq_tokens // tm
    seq_of = (lambda i: seq0) if per_seq == 0 else (lambda i: seq0 + i // per_seq)
    resident = pl.Buffered(1)
    return pl.pallas_call(
        _in_kernel,
        grid=(t // tm,),
        in_specs=[pl.BlockSpec((tm, D_MODEL), lambda i: (i, 0)),
                  _mod_spec(0, seq_of), _mod_spec(1, seq_of),
                  pl.BlockSpec((D_MODEL, MLA_COLS), lambda i: (0, 0), pipeline_mode=resident),
                  pl.BlockSpec((D_MODEL, RW_COLS), lambda i: (0, 0), pipeline_mode=resident)],
        out_specs=[pl.BlockSpec((tm, MLA_COLS), lambda i: (i, 0)),
                   pl.BlockSpec((tm, RW_COLS), lambda i: (i, 0))],
        out_shape=[jax.ShapeDtypeStruct((t, MLA_COLS), F32),
                   jax.ShapeDtypeStruct((t, RW_COLS), F32)],
        compiler_params=_cp(("arbitrary",)),
        name="in_proj",
    )(x2, mod4, mod4, w_mla, w_rw)


def _q_kernel(*refs, rope):
    if rope:
        qd_ref, g_ref, w_ref, wsw_ref, c_ref, s_ref, o_ref = refs
    else:
        qd_ref, g_ref, w_ref, o_ref = refs
    x = qd_ref[...]
    qn = (x * lax.rsqrt(jnp.mean(x * x, -1, keepdims=True) + RMS_EPS) * g_ref[...]).astype(BF16)
    q = _dot(qn, w_ref[...])
    if rope:
        qs = _dot(qn, wsw_ref[...])
        c, s = c_ref[...], s_ref[...]
        parts = []
        for h in range(MLA_HEADS):
            lo = h * QK_PAD
            parts += [q[:, lo:lo + LANES],
                      q[:, lo + LANES:lo + QK_PAD] * c + qs[:, h * LANES:(h + 1) * LANES] * s]
        q = jnp.concatenate(parts, axis=1)
    o_ref[...] = q.astype(BF16)


def _q_proj(proj_mla, n_seq_tokens, q_norm_g, wq_full, wq_sw, rope_tabs):
    t, tm = proj_mla.shape[0], 256
    per_seq = n_seq_tokens // tm
    rope = rope_tabs is not None
    width = MLA_HEADS * QK_PAD
    const = lambda i: (0, 0)
    in_specs = [pl.BlockSpec((tm, Q_LORA), lambda i: (i, 0)),
                pl.BlockSpec((1, Q_LORA), const),
                pl.BlockSpec((Q_LORA, width), const)]
    args = [proj_mla, q_norm_g.reshape(1, Q_LORA), wq_full]
    if rope:
        in_specs += [pl.BlockSpec((Q_LORA, MLA_HEADS * LANES), const),
                     pl.BlockSpec((tm, LANES), lambda i: (i % per_seq, 0)),
                     pl.BlockSpec((tm, LANES), lambda i: (i % per_seq, 0))]
        args += [wq_sw, rope_tabs[0], rope_tabs[1]]
    return pl.pallas_call(
        functools.partial(_q_kernel, rope=rope),
        grid=(t // tm,),
        in_specs=in_specs,
        out_specs=pl.BlockSpec((tm, width), lambda i: (i, 0)),
        out_shape=jax.ShapeDtypeStruct((t, width), BF16),
        compiler_params=_cp(("arbitrary",)),
        name="q_proj",
    )(*args)


def _kv_kernel(*refs, norm, rope, emit_new):
    it = iter(refs)
    ckv_ref, kr_ref = next(it), next(it)
    g_ref = next(it) if norm else None
    cs_ref = next(it) if rope else None
    wk_ref, wv_ref, e_ref = next(it), next(it), next(it)
    k_out, v_out = next(it), next(it)
    x = ckv_ref[...]
    if norm:
        x = x * lax.rsqrt(jnp.mean(x * x, -1, keepdims=True) + RMS_EPS) * g_ref[...]
    xb = x.astype(BF16)
    kr = kr_ref[...]
    if rope:
        y = kr * cs_ref[...]
        kr = y + pltpu.roll(y, QK_ROPE, 1)
    kr = kr[:, :QK_ROPE]
    k_out[...] = (_dot(xb, wk_ref[...]) + _dot(kr.astype(BF16), e_ref[...])).T.astype(BF16)
    v_out[...] = _dot(xb, wv_ref[...]).astype(BF16)
    if emit_new:
        nc_out, nk_out = next(it), next(it)
        nc_out[...] = x
        nk_out[...] = kr


def _kv_proj(ckv_src, ckv_blk, kr_src, kr_blk, kr_w, n_seq_tokens, kv_norm_g, cs_tab, wk_full, wv, e_mat, emit_new):
    t, tm = ckv_src.shape[0], 256
    per_seq = n_seq_tokens // tm
    norm, rope = kv_norm_g is not None, cs_tab is not None
    kw, vw = MLA_HEADS * QK_PAD, MLA_HEADS * V_HEAD
    const = lambda i: (0, 0)
    in_specs = [pl.BlockSpec((tm, KV_LORA), lambda i: (i, ckv_blk)),
                pl.BlockSpec((tm, kr_w), lambda i: (i, kr_blk))]
    args = [ckv_src, kr_src]
    if norm:
        in_specs.append(pl.BlockSpec((1, KV_LORA), const))
        args.append(kv_norm_g.reshape(1, KV_LORA))
    if rope:
        in_specs.append(pl.BlockSpec((tm, LANES), lambda i: (i % per_seq, 0)))
        args.append(cs_tab)
    in_specs += [pl.BlockSpec((KV_LORA, kw), const), pl.BlockSpec((KV_LORA, vw), const),
                 pl.BlockSpec((QK_ROPE, kw), const)]
    args += [wk_full, wv, e_mat]
    out_specs = [pl.BlockSpec((kw, tm), lambda i: (i // per_seq, i % per_seq)),
                 pl.BlockSpec((tm, vw), lambda i: (i, 0))]
    out_shape = [jax.ShapeDtypeStruct((t // n_seq_tokens * kw, n_seq_tokens), BF16),
                 jax.ShapeDtypeStruct((t, vw), BF16)]
    if emit_new:
        out_specs += [pl.BlockSpec((tm, KV_LORA), lambda i: (i, 0)), pl.BlockSpec((tm, QK_ROPE), lambda i: (i, 0))]
        out_shape += [jax.ShapeDtypeStruct((t, KV_LORA), F32), jax.ShapeDtypeStruct((t, QK_ROPE), F32)]
    return pl.pallas_call(
        functools.partial(_kv_kernel, norm=norm, rope=rope, emit_new=emit_new),
        grid=(t // tm,),
        in_specs=in_specs, out_specs=out_specs, out_shape=out_shape,
        compiler_params=_cp(("arbitrary",)),
        name="kv_proj",
    )(*args)


def _attn_kernel(*refs, cache, heads, exp2_scale):
    if cache:
        q_ref, k_ref, v_ref, kc_ref, vc_ref, o_ref = refs
    else:
        q_ref, k_ref, v_ref, o_ref = refs
    out = [None] * heads

    def head_stages(h):
        qk = slice(h * QK_PAD, (h + 1) * QK_PAD)
        hv = slice(h * V_HEAD, (h + 1) * V_HEAD)
        q = q_ref[:, qk]
        s = _dot(q, k_ref[qk, :])
        sc = _dot(q, kc_ref[qk, :]) if cache else None
        yield
        m = jnp.max(s, -1, keepdims=True)
        if cache:
            m = jnp.maximum(m, jnp.max(sc, -1, keepdims=True))
        yield
        p = jnp.exp2((s - m) * exp2_scale).astype(BF16)
        if cache:
            pc = jnp.exp2((sc - m) * exp2_scale).astype(BF16)
        yield
        v = v_ref[:, hv]
        acc = _dot(p, jnp.concatenate([v, jnp.ones_like(v)], axis=1))
        if cache:
            vc = vc_ref[:, hv]
            acc = acc + _dot(pc, jnp.concatenate([vc, jnp.ones_like(vc)], axis=1))
        out[h] = acc[:, :V_HEAD] * (1.0 / acc[:, V_HEAD:])
        yield

    live, waiting = [], [head_stages(h) for h in range(heads)]
    while live or waiting:
        if waiting:
            live.append(waiting.pop(0))
        live = [g for g in live if next(g, StopIteration) is not StopIteration]
    o_ref[...] = jnp.concatenate(out, axis=1).astype(BF16)


def _attention(q, kt, v, kct, vc, batch, n_tok):
    tq, heads = min(ATTN_TQ, n_tok), ATTN_HEADS
    n_q = n_tok // tq
    groups = MLA_HEADS // heads
    cache = kct is not None
    in_specs = [pl.BlockSpec((tq, heads * QK_PAD), lambda b, h, i: (b * n_q + i, h)),
                pl.BlockSpec((heads * QK_PAD, n_tok), lambda b, h, i: (b * groups + h, 0)),
                pl.BlockSpec((n_tok, heads * V_HEAD), lambda b, h, i: (b, h))]
    args = [q, kt, v]
    if cache:
        n_c = vc.shape[0] // batch
        in_specs += [pl.BlockSpec((heads * QK_PAD, n_c), lambda b, h, i: (b * groups + h, 0)),
                     pl.BlockSpec((n_c, heads * V_HEAD), lambda b, h, i: (b, h))]
        args += [kct, vc]
    scale = 1.0 / math.sqrt(QK_NOPE + QK_ROPE)
    return pl.pallas_call(
        functools.partial(_attn_kernel, cache=cache, heads=heads, exp2_scale=scale * math.log2(math.e)),
        grid=(batch, groups, n_q),
        in_specs=in_specs,
        out_specs=pl.BlockSpec((tq, heads * V_HEAD), lambda b, h, i: (b * n_q + i, h)),
        out_shape=jax.ShapeDtypeStruct((batch * n_tok, MLA_HEADS * V_HEAD), BF16),
        compiler_params=_cp(("arbitrary", "arbitrary", "arbitrary")),
        name="attention",
    )(*args)


def _pair_blockdiag(y, lane_lo):
    zero = jnp.zeros_like(y)
    return jnp.concatenate([jnp.where(lane_lo, y, zero), jnp.where(lane_lo, zero, y)], axis=0)


def _scan_kernel(*refs, tm, npair, reverse, zero_init, emit_state, finalize):
    it = iter(refs)
    blocks = {}
    cols = ("r", "k", "v", "wa") + (("gd",) if finalize else ())
    for name in cols:
        blocks[name] = (next(it), next(it), next(it), next(it))
    kk_ref, ka_ref, w0_ref, wup_ref, a0_ref, aup_ref = (next(it) for _ in range(6))
    if finalize:
        rk_ref, a0o_ref, aupo_ref, gup_ref, gng_ref, gnb_ref, yo_ref = (next(it) for _ in range(7))
    ones_ref, tri_ref = next(it), next(it)
    s0_ref = None if zero_init else next(it)
    y_ref = next(it)
    sout_ref = next(it) if emit_state else None
    s_scr = next(it)

    t_idx = pl.program_id(2)
    n_t = pl.num_programs(2)
    tile = (n_t - 1 - t_idx) if reverse else t_idx
    first_tok = tile == 0
    last_tok = tile == n_t - 1

    @pl.when(t_idx == 0)
    def _():
        if zero_init:
            s_scr[...] = jnp.zeros_like(s_scr)
        else:
            s_scr[...] = s0_ref[...]

    def shifted(name):
        main, prv, nxt, mu = blocks[name]
        u = main[...]
        row = lax.broadcasted_iota(jnp.int32, u.shape, 0)
        p_row = jnp.where(first_tok, 0.0, prv[7:8, :])
        n_row = jnp.where(last_tok, 0.0, nxt[0:1, :])
        up = jnp.where(row == 0, p_row, pltpu.roll(u, 1, 0))
        un = jnp.where(row == tm - 1, n_row, pltpu.roll(u, tm - 1, 0))
        return u + mu[...] * (0.5 * (up + un) - u)

    ones_bd = ones_ref[...]

    seg_w = ones_bd.shape[0]

    def seg(x):
        return jnp.concatenate(
            [_dot2_right(x[:, j * seg_w:(j + 1) * seg_w], ones_bd) for j in range(x.shape[1] // seg_w)], axis=1)

    r, k, v, wa = shifted("r"), shifted("k"), shifted("v"), shifted("wa")
    wab = wa.astype(BF16)
    kk = k * kk_ref[...]
    kk = kk * lax.rsqrt(jnp.maximum(seg(kk * kk), 1e-24))
    ld = _sigmoid_tanh(w0_ref[...] + _dot(jnp.tanh(wa).astype(BF16), wup_ref[...])) * (-math.exp(-0.5))
    a = _sigmoid_tanh(a0_ref[...] + _dot(wab, aup_ref[...]))
    kd = k * (1.0 + (a - 1.0) * ka_ref[...])
    b = kk * a

    c_row = lax.broadcasted_iota(jnp.int32, (CHUNK, LANES), 0)
    c_lane = lax.broadcasted_iota(jnp.int32, (CHUNK, LANES), 1)
    c_src = c_lane & (CHUNK - 1)
    lane_lo = c_lane < CHUNK
    strict = (c_src > c_row) if reverse else (c_src < c_row)
    incl = (c_src >= c_row) if reverse else (c_src <= c_row)
    eye = (c_src == c_row).astype(F32)
    bd_row = lax.broadcasted_iota(jnp.int32, (LANES, LANES), 0)
    bd_lane = lax.broadcasted_iota(jnp.int32, (LANES, LANES), 1)
    bd_mask = jnp.right_shift(bd_row, 6) == jnp.right_shift(bd_lane, 6)
    tri = tri_ref[...]

    def nn(p, y):
        return _dot(p.astype(BF16), _pair_blockdiag(y.astype(BF16), lane_lo))

    n_chunks = tm // CHUNK
    order = list(range(n_chunks - 1, -1, -1) if reverse else range(n_chunks))
    cums = [_dot2_left(tri, ld[c * CHUNK:(c + 1) * CHUNK]) for c in range(n_chunks)]
    pairs = range(npair)
    strict2 = jnp.concatenate([strict, strict], axis=1)
    incl2 = jnp.concatenate([incl, incl], axis=1)
    st = {}

    def independent_stage(c):
        sl = slice(c * CHUNK, (c + 1) * CHUNK)
        for j in pairs:
            ln = slice(j * LANES, (j + 1) * LANES)
            r_c, v_c, kk_c, kd_c, b_c, ld_c = r[sl, ln], v[sl, ln], kk[sl, ln], kd[sl, ln], b[sl, ln], ld[sl, ln]
            cum = cums[c][:, ln]
            tot = cum[0:1] if reverse else cum[CHUNK - 1:CHUNK]
            e_neg = jnp.exp(-cum)
            e_tot = jnp.exp(tot - cum)
            a_hat = -kk_c * jnp.exp(cum - ld_c)
            r_hat = r_c * jnp.exp(cum)
            ar = jnp.concatenate([a_hat, r_hat], axis=0).astype(BF16)
            bk_hat = jnp.concatenate([_pair_blockdiag((b_c * e_neg).astype(BF16), lane_lo),
                                      _pair_blockdiag((kd_c * e_neg).astype(BF16), lane_lo)], axis=0)
            g = _dot_nt(ar, bk_hat)
            gs = jnp.where(strict2, g[:CHUNK], 0.0)
            lmat = gs[:, :LANES]
            st[j, c] = dict(
                ar=ar, v=v_c.astype(BF16), v_bd=_pair_blockdiag(v_c.astype(BF16), lane_lo),
                p=eye + lmat, q=lmat, akm=gs[:, LANES:],
                rbk=jnp.where(incl2, g[CHUNK:], 0.0).astype(BF16),
                bk=jnp.concatenate([b_c * e_tot, kd_c * e_tot], axis=0).astype(BF16),
                w=jnp.exp(tot))
        yield
        for j in pairs:
            st[j, c]["q"] = nn(st[j, c]["q"], st[j, c]["q"])
        yield
        for _ in range(int(math.log2(CHUNK)) - 2):
            for j in pairs:
                d = st[j, c]
                pq = nn(jnp.concatenate([d["p"], d["q"]], axis=0), d["q"])
                d["p"], d["q"] = d["p"] + pq[:CHUNK], pq[CHUNK:]
            yield
        for j in pairs:
            d = st[j, c]
            d["p"] = d["p"] + nn(d["p"], d["q"])
            d["akv"] = _dot(d["akm"].astype(BF16), d["v_bd"])
        yield

    s_cur = [s_scr[j] for j in pairs]
    y_cells = {}

    def dependent_stage(c):
        xs = [_dot_nt(st[j, c]["ar"], s_cur[j].astype(BF16)) for j in pairs]
        yield
        u = [nn(st[j, c]["p"], xs[j][:CHUNK] + st[j, c]["akv"]) for j in pairs]
        yield
        for j in pairs:
            d = st[j, c]
            y_cells[j, c] = xs[j][CHUNK:] + _dot(
                d["rbk"],
                jnp.concatenate([_pair_blockdiag(u[j].astype(BF16), lane_lo), d["v_bd"]], axis=0))
            upd = _dot_tn(jnp.concatenate([u[j].astype(BF16), d["v"]], axis=0), d["bk"])
            s_cur[j] = s_cur[j] * d["w"] + jnp.where(bd_mask, upd, 0.0)
        yield

    def run_side_by_side(*stages):
        live = list(stages)
        while live:
            for g in list(live):
                if next(g, StopIteration) is StopIteration:
                    live.remove(g)

    def in_sequence(stages):
        for g in stages:
            yield from g

    groups = [order[i:i + SCAN_GROUP] for i in range(0, n_chunks, SCAN_GROUP)]
    for gi in range(len(groups) + 1):
        stages = [independent_stage(c) for c in groups[gi]] if gi < len(groups) else []
        if gi > 0:
            stages.append(in_sequence([dependent_stage(c) for c in groups[gi - 1]]))
        run_side_by_side(*stages)
    s_new = s_cur
    for j in range(npair):
        s_scr[j] = s_new[j]
    y = jnp.concatenate(
        [jnp.concatenate([y_cells[j, c] for c in range(n_chunks)], axis=0) for j in range(npair)], axis=1)

    if emit_state:
        @pl.when(t_idx == n_t - 1)
        def _():
            for j in range(npair):
                sout_ref[j] = s_new[j]

    if not finalize:
        y_ref[...] = y
    else:
        rk = rk_ref[...]
        a_o = _sigmoid_tanh(a0o_ref[...] + _dot(wab, aupo_ref[...]))
        kd_o = k * (1.0 + (a_o - 1.0) * ka_ref[...])
        bonus = seg(r * rk * (kd + kd_o))
        gate = _dot(_sigmoid_tanh(shifted("gd")).astype(BF16), gup_ref[...])
        yt = y + yo_ref[...]
        inv_n = 1.0 / RWKV_HEAD
        d = yt - seg(yt) * inv_n
        var = seg(d * d) * inv_n
        yn = d * lax.rsqrt(var + GN_EPS) * gng_ref[...] + gnb_ref[...]
        y_ref[...] = ((yn + bonus * v) * gate).astype(BF16)


def _scan(proj_rw, batch, n_tok, prm, direction, s0_bd, emit_state, y_other):
    reverse = direction == 1
    finalize = y_other is not None
    zero_init = s0_bd is None
    t = batch * n_tok
    tm = min(SCAN_TM, n_tok)
    npair = SCAN_PAIRS
    width = npair * LANES
    groups = HEAD_PAIRS // npair
    n_t = n_tok // tm
    row8 = tm // 8
    n_row8 = t // 8

    def tile_of(ti):
        return (n_t - 1 - ti) if reverse else ti

    def token_specs(w, cb):
        return [pl.BlockSpec((tm, w), lambda bb, p, ti: (bb * n_t + tile_of(ti), cb(p))),
                pl.BlockSpec((8, w), lambda bb, p, ti: (jnp.maximum((bb * n_t + tile_of(ti)) * row8 - 1, 0), cb(p))),
                pl.BlockSpec((8, w), lambda bb, p, ti: (jnp.minimum((bb * n_t + tile_of(ti) + 1) * row8, n_row8 - 1),
                                                        cb(p))),
                pl.BlockSpec((1, w), lambda bb, p, ti: (0, cb(p)))]

    rowvec = lambda: pl.BlockSpec((1, width), lambda bb, p, ti: (0, p))
    lora = lambda: pl.BlockSpec((LANES, width), lambda bb, p, ti: (0, p))
    const = lambda shape: pl.BlockSpec(shape, lambda bb, p, ti: (0, 0))
    state = lambda: pl.BlockSpec((None, npair, LANES, LANES), lambda bb, p, ti: (bb, p, 0, 0))
    y_spec = lambda: pl.BlockSpec((tm, width), lambda bb, p, ti: (bb * n_t + tile_of(ti), p))

    lora_blk = 3 * RWKV_WIDTH // LANES
    in_specs, args = [], []
    for name, w, cb in (("r", width, lambda p: p), ("k", width, lambda p: groups + p),
                        ("v", width, lambda p: 2 * groups + p), ("wa", LANES, lambda p: lora_blk),
                        ("gd", LANES, lambda p: lora_blk + 1)):
        if name == "gd" and not finalize:
            continue
        in_specs += token_specs(w, cb)
        args += [proj_rw, proj_rw, proj_rw, prm["mu"]]
    d, o = prm["dirs"][direction], prm["dirs"][1 - direction]
    in_specs += [rowvec(), rowvec(), rowvec(), lora(), rowvec(), lora()]
    args += [prm["k_k"], prm["k_a"], d["w0"], d["w_up"], d["a0"], d["a_up"]]
    if finalize:
        in_specs += [rowvec(), rowvec(), lora(), lora(), rowvec(), rowvec(), y_spec()]
        args += [prm["r_k"], o["a0"], o["a_up"], prm["g_up"], prm["gn_g"], prm["gn_b"], y_other]
    in_specs += [const(prm["ones_bd"].shape), const((CHUNK, CHUNK))]
    args += [prm["ones_bd"], prm["tri"][direction]]
    if not zero_init:
        in_specs.append(state())
        args.append(s0_bd)
    out_specs = [y_spec()]
    out_shape = [jax.ShapeDtypeStruct((t, RWKV_WIDTH), BF16 if finalize else F32)]
    if emit_state:
        out_specs.append(state())
        out_shape.append(jax.ShapeDtypeStruct((batch, HEAD_PAIRS, LANES, LANES), F32))
    out = pl.pallas_call(
        functools.partial(_scan_kernel, tm=tm, npair=npair, reverse=reverse, zero_init=zero_init,
                          emit_state=emit_state, finalize=finalize),
        grid=(batch, groups, n_t),
        in_specs=in_specs, out_specs=out_specs, out_shape=out_shape,
        scratch_shapes=[pltpu.VMEM((npair, LANES, LANES), F32)],
        compiler_params=_cp(("arbitrary", "arbitrary", "arbitrary")),
        name="rwkv_scan_bwd" if reverse else "rwkv_scan_fwd",
    )(*args)
    return out if emit_state else (out[0], None)


def _out_kernel(att_ref, rw_ref, x_ref, g_ref, wa_ref, wr_ref, lg_ref, lb_ref, o_ref):
    half = o_ref.shape[0] // 2
    for rows in (slice(0, half), slice(half, 2 * half)):
        f = _dot(att_ref[rows, :], wa_ref[...]) + _dot(rw_ref[rows, :], wr_ref[...])
        o_ref[rows, :] = _layer_norm(ALPHA * x_ref[rows, :] + g_ref[...] * f, lg_ref[...], lb_ref[...])


def _out_proj(att, rw, x2, mod4, seq0, n_seq_tokens, w_att, w_rw, ln_g, ln_b):
    t, tm = x2.shape[0], min(OUT_TM, n_seq_tokens)
    per_seq = n_seq_tokens // tm
    seq_of = lambda i: seq0 + i // per_seq
    half = D_MODEL // 2
    const = lambda i: (0, 0)
    return pl.pallas_call(
        _out_kernel,
        grid=(t // tm,),
        in_specs=[pl.BlockSpec((tm, half), lambda i: (i, 0)), pl.BlockSpec((tm, half), lambda i: (i, 0)),
                  pl.BlockSpec((tm, D_MODEL), lambda i: (i, 0)), _mod_spec(2, seq_of),
                  pl.BlockSpec((half, D_MODEL), const), pl.BlockSpec((half, D_MODEL), const),
                  pl.BlockSpec((1, D_MODEL), const), pl.BlockSpec((1, D_MODEL), const)],
        out_specs=pl.BlockSpec((tm, D_MODEL), lambda i: (i, 0)),
        out_shape=jax.ShapeDtypeStruct((t, D_MODEL), F32),
        compiler_params=_cp(("arbitrary",)),
        name="out_proj",
    )(att, rw, x2, mod4, w_att, w_rw, ln_g.reshape(1, D_MODEL), ln_b.reshape(1, D_MODEL))


def _ffn_kernel(x_ref, sh_ref, sc_ref, g_ref, wg_ref, wu_ref, wd_ref, lg_ref, lb_ref, o_ref, h_scr, acc_scr):
    kf = pl.program_id(1)

    @pl.when(kf == 0)
    def _():
        h_scr[...] = (x_ref[...] * (1.0 + sc_ref[...]) + sh_ref[...]).astype(BF16)
        acc_scr[...] = jnp.zeros_like(acc_scr)

    h = h_scr[...]
    gate = _dot(h, wg_ref[...])
    act = (gate * _sigmoid(gate) * _dot(h, wu_ref[...])).astype(BF16)
    acc_scr[...] += _dot(act, wd_ref[...])

    @pl.when(kf == pl.num_programs(1) - 1)
    def _():
        o_ref[...] = _layer_norm(ALPHA * x_ref[...] + g_ref[...] * acc_scr[...], lg_ref[...], lb_ref[...])


def _ffn(x1, mod4, seq0, n_seq_tokens, w_gate, w_up, w_down, ln_g, ln_b):
    t, tf = x1.shape[0], FFN_TF
    tm = min(FFN_TM, n_seq_tokens)
    per_seq = n_seq_tokens // tm
    seq_of = lambda i, kf: seq0 + i // per_seq
    const = lambda i, kf: (0, 0)
    return pl.pallas_call(
        _ffn_kernel,
        grid=(t // tm, D_FF // tf),
        in_specs=[pl.BlockSpec((tm, D_MODEL), lambda i, kf: (i, 0)),
                  _mod_spec(3, seq_of), _mod_spec(4, seq_of), _mod_spec(5, seq_of),
                  pl.BlockSpec((D_MODEL, tf), lambda i, kf: (0, kf)),
                  pl.BlockSpec((D_MODEL, tf), lambda i, kf: (0, kf)),
                  pl.BlockSpec((tf, D_MODEL), lambda i, kf: (kf, 0)),
                  pl.BlockSpec((1, D_MODEL), const), pl.BlockSpec((1, D_MODEL), const)],
        out_specs=pl.BlockSpec((tm, D_MODEL), lambda i, kf: (i, 0)),
        out_shape=jax.ShapeDtypeStruct((t, D_MODEL), F32),
        scratch_shapes=[pltpu.VMEM((tm, D_MODEL), BF16), pltpu.VMEM((tm, D_MODEL), F32)],
        compiler_params=_cp(("arbitrary", "arbitrary")),
        name="ffn",
    )(x1, mod4, mod4, mod4, w_gate, w_up, w_down, ln_g.reshape(1, D_MODEL), ln_b.reshape(1, D_MODEL))


def _rope_tables(n_tokens):
    rows = n_tokens // GRID_W
    row = jnp.repeat(jnp.arange(rows), GRID_W).astype(F32)
    col = jnp.tile(jnp.arange(GRID_W), rows).astype(F32)
    freqs = ROPE_THETA ** (-jnp.arange(ROPE_AXIS_FREQS, dtype=F32) / ROPE_AXIS_FREQS)
    ang = jnp.concatenate([row[:, None] * freqs, col[:, None] * freqs], -1)
    cos, sin = jnp.cos(ang), jnp.sin(ang)
    cos2 = jnp.concatenate([cos, cos], -1)
    sin2 = jnp.concatenate([-sin, sin], -1)
    pad = jnp.zeros((n_tokens, LANES - QK_ROPE), F32)
    q_cos = jnp.concatenate([cos2, pad], -1)
    q_sin = jnp.concatenate([sin2, pad], -1)
    k_tab = jnp.concatenate([cos2, sin2], -1)
    return q_cos, q_sin, k_tab


def _swap_halves(w):
    half = w.shape[-1] // 2
    return jnp.concatenate([w[..., half:], w[..., :half]], -1)


def _block_diag_state(s):
    bsz = s.shape[0]
    s = s.reshape(bsz, HEAD_PAIRS, 2, RWKV_HEAD, RWKV_HEAD)
    z = jnp.zeros_like(s[:, :, 0])
    top = jnp.concatenate([s[:, :, 0], z], -1)
    bot = jnp.concatenate([z, s[:, :, 1]], -1)
    return jnp.concatenate([top, bot], -2)


def _unblock_diag_state(s_bd):
    bsz = s_bd.shape[0]
    s = jnp.stack([s_bd[:, :, :RWKV_HEAD, :RWKV_HEAD], s_bd[:, :, RWKV_HEAD:, RWKV_HEAD:]], 2)
    return s.reshape(bsz, RWKV_HEADS, RWKV_HEAD, RWKV_HEAD)


def _layer(x2, batch, n_tok, mod4, seq0, per_seq_mod, wts, rope_tabs, cache, states, is_context):
    mod_tokens = n_tok if per_seq_mod else 0
    proj_mla, proj_rw = _in_proj(x2, mod4, seq0, mod_tokens, wts["w_mla"], wts["w_rw"])
    q = _q_proj(proj_mla, n_tok, wts["q_norm_g"], wts["wq_full"], wts["wq_sw"],
                None if rope_tabs is None else rope_tabs[:2])
    kv = _kv_proj(proj_mla, OFF_KV // KV_LORA, proj_mla, OFF_KR // LANES, LANES, n_tok, wts["kv_norm_g"],
                  None if rope_tabs is None else rope_tabs[2], wts["wk_full"], wts["w_uv"], wts["e_mat"], is_context)
    kc = vc = None
    if cache is not None:
        kc, vc = _kv_proj(cache[0], 0, cache[1], 0, QK_ROPE, cache[0].shape[0] // batch, None, None,
                          wts["wk_full"], wts["w_uv"], wts["e_mat"], False)
    att = _attention(q, kv[0], kv[1], kc, vc, batch, n_tok)
    s_f = None if states is None else _block_diag_state(states[0])
    s_b = None if states is None else _block_diag_state(states[1])
    y_f, new_f = _scan(proj_rw, batch, n_tok, wts["rwkv"], 0, s_f, is_context, None)
    rw, new_b = _scan(proj_rw, batch, n_tok, wts["rwkv"], 1, s_b, is_context, y_f)
    seq_tokens = n_tok if per_seq_mod else x2.shape[0]
    x1 = _out_proj(att, rw, x2, mod4, seq0, seq_tokens, wts["w_out_att"], wts["w_out_rw"], wts["ln1_g"], wts["ln1_b"])
    y = _ffn(x1, mod4, seq0, seq_tokens, wts["w_ffn_gate"], wts["w_ffn_up"], wts["w_ffn_down"],
             wts["ln2_g"], wts["ln2_b"])
    return y, kv, new_f, new_b


def kernel(x_prompt, x_sample, cache_ckv, cache_krope, state_wkv_fwd, state_wkv_bwd, c, c_ctx, w_mod, b_mod, w_in, q_norm_g, kv_norm_g, w_uq, w_uk, w_uv, tok_shift_mu, w0_fwd, w_up_fwd, a0_fwd, a_up_fwd, w0_bwd, w_up_bwd, a0_bwd, a_up_bwd, g_up, k_k, k_a, r_k, gn_g, gn_b, w_out, ln1_g, ln1_b, w_ffn_gate, w_ffn_up, w_ffn_down, ln2_g, ln2_b):
    batch, seq = x_prompt.shape[:2]
    dec_batch, dec_seq = x_sample.shape[:2]
    layer = 0

    mod_rows = 16
    cond = jnp.concatenate([c, c_ctx[None, :], jnp.zeros((mod_rows - dec_batch - 1, D_MODEL), F32)], 0)
    mod4 = _modulation(cond, w_mod[layer], b_mod[layer]).reshape(mod_rows, 6, 1, D_MODEL)

    wi = w_in[layer].astype(BF16)
    kr_w = wi[:, OFF_KR:OFF_RW]
    uq = w_uq[layer].reshape(Q_LORA, MLA_HEADS, QK_NOPE + QK_ROPE)
    uq_pad = jnp.zeros((Q_LORA, MLA_HEADS, QK_PAD - QK_NOPE - QK_ROPE), F32)
    wq_full = jnp.concatenate([uq, uq_pad], -1).reshape(Q_LORA, MLA_HEADS * QK_PAD)
    wq_sw = jnp.concatenate([_swap_halves(uq[..., QK_NOPE:]), uq_pad], -1).reshape(Q_LORA, MLA_HEADS * LANES)
    uk = w_uk[layer].reshape(KV_LORA, MLA_HEADS, QK_NOPE)
    wk_full = jnp.concatenate([uk, jnp.zeros((KV_LORA, MLA_HEADS, QK_PAD - QK_NOPE), F32)], -1)
    e_head = jnp.concatenate([jnp.zeros((QK_ROPE, QK_NOPE), F32), jnp.eye(QK_ROPE, dtype=F32),
                              jnp.zeros((QK_ROPE, QK_PAD - QK_NOPE - QK_ROPE), F32)], -1)
    lora_pad = jnp.zeros((DECAY_LORA, RWKV_WIDTH), F32)
    row = lambda vec: vec[layer].reshape(1, -1)
    head_of = jnp.arange(2 * LANES) // RWKV_HEAD
    tri_lo = jnp.tril(jnp.ones((CHUNK, CHUNK), F32))
    rwkv = {
        "mu": row(tok_shift_mu), "k_k": row(k_k), "k_a": row(k_a), "r_k": row(r_k),
        "gn_g": row(gn_g), "gn_b": row(gn_b), "g_up": g_up[layer].astype(BF16),
        "dirs": [
            {"w0": row(w0_fwd), "a0": row(a0_fwd),
             "w_up": jnp.concatenate([w_up_fwd[layer], lora_pad], 0).astype(BF16),
             "a_up": jnp.concatenate([lora_pad, a_up_fwd[layer]], 0).astype(BF16)},
            {"w0": row(w0_bwd), "a0": row(a0_bwd),
             "w_up": jnp.concatenate([w_up_bwd[layer], lora_pad], 0).astype(BF16),
             "a_up": jnp.concatenate([lora_pad, a_up_bwd[layer]], 0).astype(BF16)}],
        "ones_bd": (head_of[:, None] == head_of[None, :]).astype(BF16),
        "tri": [tri_lo.astype(BF16), tri_lo.T.astype(BF16)],
    }
    wts = {
        "w_mla": jnp.concatenate([wi[:, :OFF_RW], _swap_halves(kr_w)], -1),
        "w_rw": wi[:, OFF_RW:],
        "q_norm_g": q_norm_g[layer], "kv_norm_g": kv_norm_g[layer],
        "wq_full": wq_full.astype(BF16), "wq_sw": wq_sw.astype(BF16),
        "wk_full": wk_full.reshape(KV_LORA, MLA_HEADS * QK_PAD).astype(BF16),
        "w_uv": w_uv[layer].astype(BF16),
        "e_mat": jnp.tile(e_head, (1, MLA_HEADS)).astype(BF16),
        "rwkv": rwkv,
        "w_out_att": w_out[layer][:MLA_HEADS * V_HEAD].astype(BF16),
        "w_out_rw": w_out[layer][MLA_HEADS * V_HEAD:].astype(BF16),
        "ln1_g": ln1_g[layer], "ln1_b": ln1_b[layer],
        "w_ffn_gate": w_ffn_gate[layer].astype(BF16), "w_ffn_up": w_ffn_up[layer].astype(BF16),
        "w_ffn_down": w_ffn_down[layer].astype(BF16),
        "ln2_g": ln2_g[layer], "ln2_b": ln2_b[layer],
    }

    y_prompt, kv_ctx, s_f, s_b = _layer(
        x_prompt.reshape(batch * seq, D_MODEL), batch, seq, mod4, dec_batch, False, wts, None, None, None, True)
    past = cache_ckv.shape[2]
    cache = (cache_ckv[:, layer].reshape(dec_batch * past, KV_LORA),
             cache_krope[:, layer].reshape(dec_batch * past, QK_ROPE))
    y_sample, _, _, _ = _layer(
        x_sample.reshape(dec_batch * dec_seq, D_MODEL), dec_batch, dec_seq, mod4, 0, True, wts,
        _rope_tables(dec_seq), cache, (state_wkv_fwd[:, layer], state_wkv_bwd[:, layer]), False)

    return (y_prompt.reshape(batch, seq, D_MODEL),
            y_sample.reshape(dec_batch, dec_seq, D_MODEL),
            kv_ctx[2].reshape(batch, 1, seq, KV_LORA),
            kv_ctx[3].reshape(batch, 1, seq, QK_ROPE),
            _unblock_diag_state(s_f)[:, None],
            _unblock_diag_state(s_b)[:, None])
```
